```python
import math
import jax, jax.numpy as jnp
from jax import lax
import numpy as np

D_MODEL = 1024
BATCH = 2
SEQ = 16384
DEPTH = 4

N_A = DEPTH // 2
N_B = DEPTH - N_A
A_HEAD = 64
A_HEADS = D_MODEL // A_HEAD
LORA_W = 64
LORA_A = 64
LORA_V = 32
GN_EPS = 64e-5
B_QK = 64
B_HEADS = D_MODEL // (2 * B_QK)
B_V = 2 * B_QK
B_WIDTH = B_HEADS * B_V
ROT_DIM = B_QK // 4
ROPE_THETA = 500000.0
Q_BLOCK = 128
NORM_EPS = 1e-6
SUBLN_EPS = 1e-5

kernel_name = "yoco_rwkv7_diffattn_hybrid"


def rmsnorm(x, g, eps=NORM_EPS):
    xf = x.astype(jnp.float32)
    y = xf * lax.rsqrt(jnp.mean(xf * xf, axis=-1, keepdims=True) + eps)
    return (y * g.astype(jnp.float32)).astype(x.dtype)


def partial_rope(t, positions):
    half = ROT_DIM // 2
    inv_freq = ROPE_THETA ** (-jnp.arange(0, ROT_DIM, 2, dtype=jnp.float32) / ROT_DIM)
    ang = positions.astype(jnp.float32)[:, :, None] * inv_freq[None, None, :]
    cos = jnp.cos(ang)[:, :, None, None, :]
    sin = jnp.sin(ang)[:, :, None, None, :]
    tf = t.astype(jnp.float32)
    t1, t2, rest = tf[..., :half], tf[..., half:ROT_DIM], tf[..., ROT_DIM:]
    out = jnp.concatenate([t1 * cos - t2 * sin, t2 * cos + t1 * sin, rest], axis=-1)
    return out.astype(t.dtype)


def wkv7_scan(r, w, k, v, a, b):
    Bn, _, H, N = r.shape

    def step(state, inp):
        r_t, w_t, k_t, v_t, a_t, b_t = inp
        sa = jnp.einsum('bhij,bhj->bhi', state, a_t)
        state = (state * w_t[:, :, None, :] + sa[..., None] * b_t[:, :, None, :]
                 + v_t[..., None] * k_t[:, :, None, :])
        y_t = jnp.einsum('bhij,bhj->bhi', state, r_t)
        return state, y_t

    xs = (jnp.moveaxis(r, 1, 0), jnp.moveaxis(w, 1, 0), jnp.moveaxis(k, 1, 0),
          jnp.moveaxis(v, 1, 0), jnp.moveaxis(a, 1, 0), jnp.moveaxis(b, 1, 0))
    s0 = jnp.zeros((Bn, H, N, N), jnp.float32)
    _, y = lax.scan(step, s0, xs)
    return jnp.moveaxis(y, 0, 1)


def rwkv7_mixer(xn, mu, w_in, w0, w1, w2, a0, a1, a2, k_k, k_a, r_k, ln_w, ln_b, w_out,
                v_first, v_res):
    Bn, S, D = xn.shape
    f32 = jnp.float32
    x_prev = jnp.pad(xn, ((0, 0), (1, 0), (0, 0)))[:, :-1]
    xx = x_prev - xn
    xm = xn[:, :, None, :] + xx[:, :, None, :] * mu[:4]
    rkvg = jnp.einsum('bspd,pde->bspe', xm, w_in)
    r, k, v, g = rkvg[:, :, 0], rkvg[:, :, 1], rkvg[:, :, 2], rkvg[:, :, 3]
    xw = xn + xx * mu[4]
    xa = xn + xx * mu[5]
    w = -jax.nn.softplus(-(w0 + jnp.tanh(xw @ w1) @ w2)) - 0.5
    a = jax.nn.sigmoid(a0 + (xa @ a1) @ a2)
    if v_res is None:
        v_first = v
    else:
        v0, v1, v2 = v_res
        v = v + (v_first - v) * jax.nn.sigmoid(v0 + (xm[:, :, 2] @ v1) @ v2)

    def hd(t):
        return t.reshape(Bn, S, A_HEADS, A_HEAD).astype(f32)

    kk = hd(k * k_k)
    kk = kk / jnp.maximum(jnp.sqrt(jnp.sum(kk * kk, axis=-1, keepdims=True)), 1e-12)
    kmod = hd(k * (1.0 + (a - 1.0) * k_a))
    ah = hd(a)
    rh = hd(r)
    vh = hd(v)
    decay = jnp.exp(-jnp.exp(hd(w)))
    y = wkv7_scan(rh, decay, kmod, vh, -kk, kk * ah)
    mean = jnp.mean(y, axis=-1, keepdims=True)
    var = jnp.mean(jnp.square(y - mean), axis=-1, keepdims=True)
    yn = (y - mean) * lax.rsqrt(var + GN_EPS)
    yn = yn * ln_w.reshape(A_HEADS, A_HEAD).astype(f32) + ln_b.reshape(A_HEADS, A_HEAD).astype(f32)
    bonus = jnp.sum(rh * kmod * r_k.astype(f32), axis=-1, keepdims=True) * vh
    o = (yn + bonus).reshape(Bn, S, D).astype(xn.dtype) * jax.nn.silu(g)
    return o @ w_out, v_first


def diff_attention(q, k, v, lam):
    Bn, S, H = q.shape[:3]
    scale = B_QK ** -0.5
    qh = q.transpose(0, 2, 3, 1, 4)
    kh = k.transpose(0, 2, 3, 1, 4)
    vh = v.transpose(0, 2, 1, 3)
    n_blk = S // Q_BLOCK
    kpos = jnp.arange(S)

    def one_block(i):
        q0 = i * Q_BLOCK
        qb = lax.dynamic_slice_in_dim(qh, q0, Q_BLOCK, axis=3)
        s = jnp.einsum('bhmqd,bhmkd->bhmqk', qb, kh, preferred_element_type=jnp.float32) * scale
        qpos = q0 + jnp.arange(Q_BLOCK)
        s = jnp.where(kpos[None, :] <= qpos[:, None], s, -jnp.inf)
        p = jax.nn.softmax(s, axis=-1)
        attn = p[:, :, 0] - lam * p[:, :, 1]
        return jnp.einsum('bhqk,bhkd->bhqd', attn.astype(vh.dtype), vh)

    o = lax.map(one_block, jnp.arange(n_blk))
    return o.transpose(1, 0, 3, 2, 4).reshape(Bn, S, H, B_V)


def setup_inputs(seed: int = 0) -> dict:
    key = jax.random.key(seed)
    ks = iter(jax.random.split(key, 40))
    f32 = jnp.float32
    D = D_MODEL

    def nrm(shape, scale):
        return jax.random.normal(next(ks), shape, f32) * scale

    def unif(shape, lo, hi):
        return jax.random.uniform(next(ks), shape, f32, lo, hi)

    x = jax.random.normal(next(ks), (BATCH, SEQ, D), f32)
    positions = jnp.broadcast_to(jnp.arange(SEQ, dtype=jnp.int32)[None, :], (BATCH, SEQ))
    inp = {
        "x": x,
        "positions": positions,
        "a_norm": 1.0 + nrm((N_A, D), 0.02),
        "a_mu": unif((N_A, 6, D), 0.0, 1.0),
        "a_w_in": nrm((N_A, 4, D, D), D ** -0.5),
        "a_w0": unif((N_A, D), -6.0, -1.0),
        "a_w1": nrm((N_A, D, LORA_W), D ** -0.5),
        "a_w2": nrm((N_A, LORA_W, D), 0.1 * LORA_W ** -0.5),
        "a_a0": nrm((N_A, D), 0.1),
        "a_a1": nrm((N_A, D, LORA_A), D ** -0.5),
        "a_a2": nrm((N_A, LORA_A, D), 0.1 * LORA_A ** -0.5),
        "a_v0": nrm((N_A - 1, D), 0.1),
        "a_v1": nrm((N_A - 1, D, LORA_V), D ** -0.5),
        "a_v2": nrm((N_A - 1, LORA_V, D), 0.1 * LORA_V ** -0.5),
        "a_k_k": 0.85 + nrm((N_A, D), 0.05),
        "a_k_a": 1.0 + nrm((N_A, D), 0.05),
        "a_r_k": nrm((N_A, A_HEADS, A_HEAD), 0.1),
        "a_ln_w": 1.0 + nrm((N_A, D), 0.02),
        "a_ln_b": nrm((N_A, D), 0.02),
        "a_w_out": nrm((N_A, D, D), D ** -0.5),
        "kv_norm": 1.0 + nrm((D,), 0.02),
        "w_kv": nrm((D, B_HEADS * 2 * B_QK + B_WIDTH), D ** -0.5),
        "b_norm": 1.0 + nrm((N_B, D), 0.02),
        "b_w_in": nrm((N_B, D, B_HEADS * 2 * B_QK + B_WIDTH), D ** -0.5),
        "b_lq1": nrm((N_B, B_QK), 0.1),
        "b_lk1": nrm((N_B, B_QK), 0.1),
        "b_lq2": nrm((N_B, B_QK), 0.1),
        "b_lk2": nrm((N_B, B_QK), 0.1),
        "b_subln": 1.0 + nrm((N_B, B_V), 0.02),
        "b_w_out": nrm((N_B, B_WIDTH, D), B_WIDTH ** -0.5),
        "final_norm": 1.0 + nrm((D,), 0.02),
    }
    return inp


def reference(x, positions, a_norm, a_mu, a_w_in, a_w0, a_w1, a_w2, a_a0, a_a1, a_a2,
              a_v0, a_v1, a_v2, a_k_k, a_k_a, a_r_k, a_ln_w, a_ln_b, a_w_out,
              kv_norm, w_kv, b_norm, b_w_in, b_lq1, b_lk1, b_lq2, b_lk2, b_subln, b_w_out,
              final_norm):
    Bn, S, D = x.shape
    nq = B_HEADS * 2 * B_QK
    v_first = None
    k_shared = None
    v_shared = None
    for layer in range(DEPTH):
        if layer < N_A:
            xn = rmsnorm(x, a_norm[layer])
            v_res = None if layer == 0 else (a_v0[layer - 1], a_v1[layer - 1], a_v2[layer - 1])
            o, v_first = rwkv7_mixer(xn, a_mu[layer], a_w_in[layer], a_w0[layer], a_w1[layer],
                                     a_w2[layer], a_a0[layer], a_a1[layer], a_a2[layer],
                                     a_k_k[layer], a_k_a[layer], a_r_k[layer], a_ln_w[layer],
                                     a_ln_b[layer], a_w_out[layer], v_first, v_res)
            x = x + o
        else:
            j = layer - N_A
            if k_shared is None:
                h = rmsnorm(x, kv_norm)
                kv = h @ w_kv
                k_shared = partial_rope(kv[..., :nq].reshape(Bn, S, B_HEADS, 2, B_QK), positions)
                v_shared = kv[..., nq:].reshape(Bn, S, B_HEADS, B_V)
            xn = rmsnorm(x, b_norm[j])
            proj = xn @ b_w_in[j]
            q = partial_rope(proj[..., :nq].reshape(Bn, S, B_HEADS, 2, B_QK), positions)
            gate = proj[..., nq:]
            lam_init = 0.8 - 0.6 * math.exp(-0.3 * layer)
            lam = (jnp.exp(jnp.sum(b_lq1[j].astype(jnp.float32) * b_lk1[j].astype(jnp.float32)))
                   - jnp.exp(jnp.sum(b_lq2[j].astype(jnp.float32) * b_lk2[j].astype(jnp.float32)))
                   + lam_init)
            o = diff_attention(q, k_shared, v_shared, lam)
            o = rmsnorm(o, b_subln[j], SUBLN_EPS) * (1.0 - lam_init)
            o = o.reshape(Bn, S, B_WIDTH).astype(x.dtype) * jax.nn.silu(gate)
            x = x + o @ b_w_out[j]
    return rmsnorm(x, final_norm)
```

```python
import functools
import math

import jax
import jax.numpy as jnp
from jax import lax
from jax.experimental import pallas as pl
from jax.experimental.pallas import tpu as pltpu

F32 = jnp.float32
BF16 = jnp.bfloat16

A_HEAD = 64
B_QK = 64
B_V = 128
ROT_DIM = B_QK // 4
ROPE_THETA = 500000.0
NORM_EPS = 1e-6
SUBLN_EPS = 1e-5
GN_EPS = 64e-5
N_A = 2

LANES = 128
VMEM_LIMIT = 48 * 1024 * 1024

CHUNK = 64
TM = 256
BQ = 256
BK = 256

_MU, _NORM, _W0, _A0, _V0, _KK, _KA, _RK, _LNW, _LNB = 0, 6, 7, 8, 9, 10, 11, 12, 13, 14


def _params(sem):
    return pltpu.CompilerParams(dimension_semantics=sem, vmem_limit_bytes=VMEM_LIMIT)


def _bdot(a, b):
    return jnp.dot(a.astype(BF16), b.astype(BF16), preferred_element_type=F32)


def _nt(a, b):
    return lax.dot_general(a.astype(BF16), b.astype(BF16), (((1,), (1,)), ((), ())),
                           preferred_element_type=F32)


def _tn(a, b):
    return lax.dot_general(a.astype(BF16), b.astype(BF16), (((0,), (0,)), ((), ())),
                           preferred_element_type=F32)


def _dot_hi_lo(x, m):
    hi = x.astype(BF16)
    lo = (x - hi.astype(F32)).astype(BF16)
    return (jnp.dot(hi, m, preferred_element_type=F32) + jnp.dot(lo, m, preferred_element_type=F32))


def _rms(x, g, eps):
    return x * lax.rsqrt(jnp.mean(x * x, axis=-1, keepdims=True) + eps) * g


def _sigmoid(x):
    return 1.0 / (1.0 + jnp.exp(-x))


def _rwkv_pre_kernel(has_vres, *refs):
    if has_vres:
        (x_ref, vec_ref, win_ref, w1_ref, w2_ref, a1_ref, a2_ref, e_ref, et_ref,
         v1_ref, v2_ref, vf_ref,
         r_out, lw_out, k_out, v_out, kk_out, b_out, g_out, carry) = refs
    else:
        (x_ref, vec_ref, win_ref, w1_ref, w2_ref, a1_ref, a2_ref, e_ref, et_ref,
         r_out, lw_out, k_out, v_out, kk_out, b_out, g_out, carry) = refs

    def vec(i):
        return vec_ref[i:i + 1, :]

    @pl.when(pl.program_id(1) == 0)
    def _():
        carry[...] = jnp.zeros_like(carry)

    xn = _rms(x_ref[...], vec(_NORM), NORM_EPS)
    tm = xn.shape[0]
    rolled = pltpu.roll(xn, 1, axis=0)
    row = lax.broadcasted_iota(jnp.int32, (tm, 1), 0)
    prev = jnp.where(row == 0, carry[7:8, :], rolled)
    carry[...] = xn[tm - 8:, :]
    xx = prev - xn

    def mix(p):
        return xn + xx * vec(_MU + p)

    r = _bdot(mix(0), win_ref[0])
    k = _bdot(mix(1), win_ref[1])
    xm_v = mix(2)
    v = _bdot(xm_v, win_ref[2])
    g_out[...] = _bdot(mix(3), win_ref[3])

    wl = vec(_W0) + _bdot(jnp.tanh(_bdot(mix(4), w1_ref[...])), w2_ref[...])
    w = -jax.nn.softplus(-wl) - 0.5
    lw_out[...] = -jnp.exp(w)
    lr = _sigmoid(vec(_A0) + _bdot(_bdot(mix(5), a1_ref[...]), a2_ref[...]))
    if has_vres:
        gate_v = _sigmoid(vec(_V0) + _bdot(_bdot(xm_v, v1_ref[...]), v2_ref[...]))
        v = v + (vf_ref[...] - v) * gate_v

    kk = k * vec(_KK)
    ss = _dot_hi_lo(kk * kk, e_ref[...])
    inv = 1.0 / jnp.maximum(jnp.sqrt(ss), 1e-12)
    kk = kk * _dot_hi_lo(inv, et_ref[...])
    r_out[...] = r
    v_out[...] = v
    kk_out[...] = kk
    b_out[...] = kk * lr
    k_out[...] = k * (1.0 + (lr - 1.0) * vec(_KA))


def _rwkv_pre(x, vecs, w_in, w1, w2, a1, a2, seg, seg_t, vres):
    bsz, seq, d = x.shape
    tok = pl.BlockSpec((None, TM, d), lambda b, j: (b, j, 0))

    def full(a):
        return pl.BlockSpec(a.shape, lambda b, j: (0,) * a.ndim)

    args = [x, vecs, w_in, w1, w2, a1, a2, seg, seg_t]
    specs = [tok] + [full(a) for a in args[1:]]
    if vres is not None:
        v1, v2, v_first = vres
        args += [v1, v2, v_first]
        specs += [full(v1), full(v2), tok]
    out = jax.ShapeDtypeStruct((bsz, seq, d), F32)
    return pl.pallas_call(
        functools.partial(_rwkv_pre_kernel, vres is not None),
        grid=(bsz, seq // TM),
        in_specs=specs,
        out_specs=[tok] * 7,
        out_shape=[out] * 7,
        scratch_shapes=[pltpu.VMEM((8, d), F32)],
        compiler_params=_params(("arbitrary", "arbitrary")),
        name="rwkv_pre",
    )(*args)


def _stack(x, lane_head):
    zero = jnp.zeros_like(x)
    return jnp.concatenate([jnp.where(lane_head == 0, x, zero),
                            jnp.where(lane_head == 1, x, zero)], axis=0).astype(BF16)


def _wkv_pair(r, lw, k, v, kk, b, state, tri):
    n = 2 * CHUNK
    lane_head = lax.broadcasted_iota(jnp.int32, (CHUNK, LANES), 1) // A_HEAD
    h1 = lw.astype(BF16)
    r1 = lw - h1.astype(F32)
    h2 = r1.astype(BF16)
    h3 = (r1 - h2.astype(F32)).astype(BF16)
    cl = (jnp.dot(tri, h1, preferred_element_type=F32) + jnp.dot(tri, h2, preferred_element_type=F32)
          + jnp.dot(tri, h3, preferred_element_type=F32))
    e_neg = jnp.exp(-cl)
    e_end = jnp.exp(cl[CHUNK - 1:CHUNK, :])
    a_s = _stack(-kk * jnp.exp(cl - lw), lane_head)
    r_s = _stack(r * jnp.exp(cl), lane_head)
    bt = b * e_neg
    kt = k * e_neg
    b_s = _stack(bt, lane_head)
    k_s = _stack(kt, lane_head)
    v_s = _stack(v, lane_head)
    bh_s = _stack(bt * e_end, lane_head)
    kh_s = _stack(kt * e_end, lane_head)

    sc = _nt(jnp.concatenate([a_s, r_s], axis=0), jnp.concatenate([b_s, k_s], axis=0))
    row = lax.broadcasted_iota(jnp.int32, (n, n), 0) % CHUNK
    col = lax.broadcasted_iota(jnp.int32, (n, n), 1) % CHUNK
    strict = col < row
    incl = col <= row
    a_ab = jnp.where(strict, sc[:n, :n], 0.0)
    a_ak = jnp.where(strict, sc[:n, n:], 0.0)
    a_rb = jnp.where(incl, sc[n:, :n], 0.0)
    a_rk = jnp.where(incl, sc[n:, n:], 0.0)

    eye = (lax.broadcasted_iota(jnp.int32, (n, n), 0)
           == lax.broadcasted_iota(jnp.int32, (n, n), 1)).astype(F32)
    t_inv = eye + a_ab
    pw = a_ab
    for _ in range(int(math.log2(CHUNK)) - 1):
        pw = _bdot(pw, pw)
        t_inv = t_inv + _bdot(t_inv, pw)

    st = state.astype(BF16)
    u = _bdot(t_inv, _nt(a_s, st) + _bdot(a_ak, v_s))
    uv = jnp.concatenate([u.astype(BF16), v_s], axis=0)
    y = _nt(r_s, st) + _bdot(jnp.concatenate([a_rb, a_rk], axis=1), uv)
    new_state = state * e_end + _tn(uv, jnp.concatenate([bh_s, kh_s], axis=0))
    return y[:CHUNK] + y[CHUNK:], new_state


def _wkv_kernel(r_ref, lw_ref, k_ref, v_ref, kk_ref, b_ref, y_ref, state):
    @pl.when(pl.program_id(1) == 0)
    def _():
        state[...] = jnp.zeros_like(state)

    tri = (lax.broadcasted_iota(jnp.int32, (CHUNK, CHUNK), 1)
           <= lax.broadcasted_iota(jnp.int32, (CHUNK, CHUNK), 0)).astype(BF16)
    for p in range(state.shape[0]):
        sl = slice(p * LANES, (p + 1) * LANES)
        y, new_state = _wkv_pair(r_ref[:, sl], lw_ref[:, sl], k_ref[:, sl], v_ref[:, sl],
                                 kk_ref[:, sl], b_ref[:, sl], state[p], tri)
        y_ref[:, sl] = y
        state[p] = new_state


def _wkv(r, lw, k, v, kk, b):
    bsz, seq, d = r.shape
    tok = pl.BlockSpec((None, CHUNK, d), lambda bi, j: (bi, j, 0))
    return pl.pallas_call(
        _wkv_kernel,
        grid=(bsz, seq // CHUNK),
        in_specs=[tok] * 6,
        out_specs=tok,
        out_shape=jax.ShapeDtypeStruct((bsz, seq, d), F32),
        scratch_shapes=[pltpu.VMEM((d // LANES, LANES, LANES), F32)],
        compiler_params=_params(("arbitrary", "arbitrary")),
        name="wkv",
    )(r, lw, k, v, kk, b)


def _rwkv_post_kernel(y_ref, r_ref, k_ref, v_ref, g_ref, x_ref, vec_ref, wout_ref, e_ref, et_ref,
                      out_ref):
    def vec(i):
        return vec_ref[i:i + 1, :]

    seg = e_ref[...]
    seg_t = et_ref[...]

    def head_sum(t):
        return _dot_hi_lo(_dot_hi_lo(t, seg), seg_t)

    y = y_ref[...]
    yc = y - head_sum(y) * (1.0 / A_HEAD)
    var = head_sum(yc * yc) * (1.0 / A_HEAD)
    yn = yc * lax.rsqrt(var + GN_EPS) * vec(_LNW) + vec(_LNB)
    v = v_ref[...]
    bonus = head_sum(r_ref[...] * k_ref[...] * vec(_RK)) * v
    g = g_ref[...]
    o = (yn + bonus) * (g * _sigmoid(g))
    out_ref[...] = x_ref[...] + _bdot(o, wout_ref[...])


def _rwkv_post(y, r, k, v, g, x, vecs, w_out, seg, seg_t):
    t, d = x.shape
    tok = pl.BlockSpec((TM, d), lambda i: (i, 0))

    def full(a):
        return pl.BlockSpec(a.shape, lambda i: (0,) * a.ndim)

    return pl.pallas_call(
        _rwkv_post_kernel,
        grid=(t // TM,),
        in_specs=[tok] * 6 + [full(vecs), full(w_out), full(seg), full(seg_t)],
        out_specs=tok,
        out_shape=jax.ShapeDtypeStruct((t, d), F32),
        compiler_params=_params(("arbitrary",)),
        name="rwkv_post",
    )(y, r, k, v, g, x, vecs, w_out, seg, seg_t)


def _proj_rope_kernel(scale, x_ref, g_ref, w_ref, pos_ref, freq_ref, rot_ref, lin_ref):
    xn = _rms(x_ref[...], g_ref[...], NORM_EPS)
    proj = _bdot(xn, w_ref[...])
    d = rot_ref.shape[1]
    ang = pos_ref[...] * freq_ref[0:1, :]
    cos = jnp.cos(ang) * scale
    sin = jnp.sin(ang) * scale
    sin_lo = sin * freq_ref[1:2, :]
    sin_hi = sin * freq_ref[2:3, :]
    half = ROT_DIM // 2
    for c in range(d // LANES):
        t = proj[:, c * LANES:(c + 1) * LANES]
        rot = (t * cos + pltpu.roll(t, LANES - half, axis=1) * sin_lo
               + pltpu.roll(t, half, axis=1) * sin_hi)
        rot_ref[:, c * LANES:(c + 1) * LANES] = rot.astype(rot_ref.dtype)
    lin_ref[...] = proj[:, d:].astype(lin_ref.dtype)


def _proj_rope(x, g, w, pos, freq, scale, lin_dtype):
    t, d = x.shape
    tok = pl.BlockSpec((TM, d), lambda i: (i, 0))

    def full(a):
        return pl.BlockSpec(a.shape, lambda i: (0,) * a.ndim)

    return pl.pallas_call(
        functools.partial(_proj_rope_kernel, scale),
        grid=(t // TM,),
        in_specs=[tok, full(g), full(w), pl.BlockSpec((TM, LANES), lambda i: (i, 0)), full(freq)],
        out_specs=[tok, tok],
        out_shape=[jax.ShapeDtypeStruct((t, d), BF16), jax.ShapeDtypeStruct((t, d), lin_dtype)],
        compiler_params=_params(("arbitrary",)),
        name="proj_rope",
    )(x, g, w, pos, freq)


def _diff_attn_kernel(lam_init, q_ref, k_ref, v_ref, lam_ref, sub_ref, out_ref, m_scr, l_scr, acc_scr):
    i = pl.program_id(2)
    q = q_ref[...]
    lane_map = lax.broadcasted_iota(jnp.int32, q.shape, 1) // B_QK
    zero = jnp.zeros_like(q)
    qs = jnp.concatenate([jnp.where(lane_map == 0, q, zero), jnp.where(lane_map == 1, q, zero)], axis=0)

    m_scr[...] = jnp.full(m_scr.shape, -jnp.inf, F32)
    l_scr[...] = jnp.zeros_like(l_scr)
    acc_scr[...] = jnp.zeros_like(acc_scr)

    def step(j, masked):
        start = pl.multiple_of(j * BK, BK)
        kb = k_ref[pl.ds(start, BK), :]
        vb = v_ref[pl.ds(start, BK), :]
        s = lax.dot_general(qs, kb, (((1,), (1,)), ((), ())), preferred_element_type=F32)
        if masked:
            qpos = lax.broadcasted_iota(jnp.int32, s.shape, 0) % BQ
            kpos = lax.broadcasted_iota(jnp.int32, s.shape, 1)
            s = jnp.where(kpos <= qpos, s, -jnp.inf)
        m_prev = m_scr[...]
        m_new = jnp.maximum(m_prev, jnp.max(s, axis=1, keepdims=True))
        p = jnp.exp(s - jnp.tile(m_new, (1, BK // LANES)))
        alpha = jnp.exp(m_prev - m_new)
        l_scr[...] = alpha * l_scr[...] + jnp.sum(p, axis=1, keepdims=True)
        acc_scr[...] = alpha * acc_scr[...] + jnp.dot(p.astype(BF16), vb, preferred_element_type=F32)
        m_scr[...] = m_new

    def body(j, carry):
        step(j, False)
        return carry

    lax.fori_loop(0, i, body, 0)
    step(i, True)

    o_all = acc_scr[...] / l_scr[...]
    lam = (jnp.exp(jnp.sum(lam_ref[0:1, :] * lam_ref[1:2, :], axis=1, keepdims=True))
           - jnp.exp(jnp.sum(lam_ref[2:3, :] * lam_ref[3:4, :], axis=1, keepdims=True)) + lam_init)
    o = o_all[:BQ] - lam * o_all[BQ:]
    out_ref[...] = _rms(o, sub_ref[...], SUBLN_EPS) * (1.0 - lam_init)


def _diff_attn(q, k, v, lam_vecs, subln, lam_init, bsz, seq):
    t, d = q.shape
    nq = seq // BQ
    heads = d // LANES
    qspec = pl.BlockSpec((BQ, LANES), lambda b, h, i: (b * nq + i, h))
    kvspec = pl.BlockSpec((seq, LANES), lambda b, h, i: (b, h))

    def full(a):
        return pl.BlockSpec(a.shape, lambda b, h, i: (0,) * a.ndim)

    return pl.pallas_call(
        functools.partial(_diff_attn_kernel, lam_init),
        grid=(bsz, heads, nq),
        in_specs=[qspec, kvspec, kvspec, full(lam_vecs), full(subln)],
        out_specs=qspec,
        out_shape=jax.ShapeDtypeStruct((t, d), F32),
        scratch_shapes=[pltpu.VMEM((2 * BQ, LANES), F32)] * 3,
        compiler_params=_params(("arbitrary", "arbitrary", "arbitrary")),
        name="diff_attn",
    )(q, k, v, lam_vecs, subln)


def _attn_post_kernel(final, o_ref, gate_ref, x_ref, wout_ref, fin_ref, out_ref):
    gate = gate_ref[...]
    o = o_ref[...] * (gate * _sigmoid(gate))
    out = x_ref[...] + _bdot(o, wout_ref[...])
    if final:
        out = _rms(out, fin_ref[...], NORM_EPS)
    out_ref[...] = out


def _attn_post(o, gate, x, w_out, fin, final):
    t, d = x.shape
    tok = pl.BlockSpec((TM, d), lambda i: (i, 0))

    def full(a):
        return pl.BlockSpec(a.shape, lambda i: (0,) * a.ndim)

    return pl.pallas_call(
        functools.partial(_attn_post_kernel, final),
        grid=(t // TM,),
        in_specs=[tok, tok, tok, full(w_out), full(fin)],
        out_specs=tok,
        out_shape=jax.ShapeDtypeStruct((t, d), F32),
        compiler_params=_params(("arbitrary",)),
        name="attn_post",
    )(o, gate, x, w_out, fin)


def _rope_tables():
    lane = jnp.arange(LANES) % B_QK
    half = ROT_DIM // 2
    inv_freq = ROPE_THETA ** (-jnp.arange(0, ROT_DIM, 2, dtype=F32) / ROT_DIM)
    freq = jnp.where(lane < ROT_DIM, inv_freq[lane % half], 0.0)
    lo = jnp.where(lane < half, -1.0, 0.0)
    hi = jnp.where((lane >= half) & (lane < ROT_DIM), 1.0, 0.0)
    return jnp.zeros((8, LANES), F32).at[0].set(freq).at[1].set(lo).at[2].set(hi)


def kernel(x, positions, a_norm, a_mu, a_w_in, a_w0, a_w1, a_w2, a_a0, a_a1, a_a2, a_v0, a_v1, a_v2, a_k_k, a_k_a, a_r_k, a_ln_w, a_ln_b, a_w_out, kv_norm, w_kv, b_norm, b_w_in, b_lq1, b_lk1, b_lq2, b_lk2, b_subln, b_w_out, final_norm):
    bsz, seq, d = x.shape
    t = bsz * seq
    heads_a = d // A_HEAD
    seg = (jnp.arange(d)[:, None] // A_HEAD == jnp.arange(LANES)[None, :]).astype(BF16)
    seg_t = seg.T

    v_first = None
    for layer in range(N_A):
        zero = jnp.zeros((d,), F32)
        v0 = a_v0[layer - 1] if layer > 0 else zero
        vecs = jnp.stack([*a_mu[layer], a_norm[layer], a_w0[layer], a_a0[layer], v0, a_k_k[layer],
                          a_k_a[layer], a_r_k[layer].reshape(d), a_ln_w[layer], a_ln_b[layer], zero])
        vres = None
        if layer > 0:
            vres = (a_v1[layer - 1].astype(BF16), a_v2[layer - 1].astype(BF16), v_first)
        r, lw, k, v, kk, b, g = _rwkv_pre(
            x, vecs, a_w_in[layer].astype(BF16), a_w1[layer].astype(BF16), a_w2[layer].astype(BF16),
            a_a1[layer].astype(BF16), a_a2[layer].astype(BF16), seg, seg_t, vres)
        if layer == 0:
            v_first = v
        y = _wkv(r, lw, k, v, kk, b)
        flat = lambda a: a.reshape(t, d)
        x = _rwkv_post(flat(y), flat(r), flat(k), flat(v), flat(g), flat(x), vecs,
                       a_w_out[layer].astype(BF16), seg, seg_t).reshape(bsz, seq, d)

    x = x.reshape(t, d)
    pos = jnp.broadcast_to(positions.reshape(t, 1).astype(F32), (t, LANES))
    freq = _rope_tables()
    k_sh, v_sh = _proj_rope(x, kv_norm.reshape(1, d), w_kv.astype(BF16), pos, freq, 1.0, BF16)
    n_b = b_norm.shape[0]
    for j in range(n_b):
        layer = N_A + j
        lam_init = 0.8 - 0.6 * math.exp(-0.3 * layer)
        q, gate = _proj_rope(x, b_norm[j].reshape(1, d), b_w_in[j].astype(BF16), pos, freq,
                             B_QK ** -0.5, F32)
        lam_vecs = jnp.zeros((8, LANES), F32).at[:4, :B_QK].set(
            jnp.stack([b_lq1[j], b_lk1[j], b_lq2[j], b_lk2[j]]))
        o = _diff_attn(q, k_sh, v_sh, lam_vecs, b_subln[j].reshape(1, B_V), lam_init, bsz, seq)
        x = _attn_post(o, gate, x, b_w_out[j].astype(BF16), final_norm.reshape(1, d), j == n_b - 1)
    return x.reshape(bsz, seq, d)
```

```python
import functools
import math

import jax
import jax.numpy as jnp
from jax import lax
from jax.experimental import pallas as pl
from jax.experimental.pallas import tpu as pltpu

F32 = jnp.float32
BF16 = jnp.bfloat16

A_HEAD = 64
B_QK = 64
B_V = 128
ROT_DIM = B_QK // 4
ROPE_THETA = 500000.0
NORM_EPS = 1e-6
SUBLN_EPS = 1e-5
GN_EPS = 64e-5
N_A = 2

LANES = 128
VMEM_LIMIT = 48 * 1024 * 1024

CHUNK = 64
TM = 256
BQ = 512
BK = 512

_MU, _NORM, _W0, _A0, _V0, _KK, _KA, _RK, _LNW, _LNB = 0, 6, 7, 8, 9, 10, 11, 12, 13, 14


def _params(sem):
    return pltpu.CompilerParams(dimension_semantics=sem, vmem_limit_bytes=VMEM_LIMIT)


def _bdot(a, b):
    return jnp.dot(a.astype(BF16), b.astype(BF16), preferred_element_type=F32)


def _nt(a, b):
    return lax.dot_general(a.astype(BF16), b.astype(BF16), (((1,), (1,)), ((), ())),
                           preferred_element_type=F32)


def _tn(a, b):
    return lax.dot_general(a.astype(BF16), b.astype(BF16), (((0,), (0,)), ((), ())),
                           preferred_element_type=F32)


def _dot_hi_lo(x, m):
    hi = x.astype(BF16)
    lo = (x - hi.astype(F32)).astype(BF16)
    return (jnp.dot(hi, m, preferred_element_type=F32) + jnp.dot(lo, m, preferred_element_type=F32))


def _rms(x, g, eps):
    return x * lax.rsqrt(jnp.mean(x * x, axis=-1, keepdims=True) + eps) * g


def _sigmoid(x):
    return 1.0 / (1.0 + jnp.exp(-x))


def _rwkv_pre_kernel(has_vres, *refs):
    if has_vres:
        (x_ref, vec_ref, win_ref, w1_ref, w2_ref, a1_ref, a2_ref, e_ref, et_ref,
         v1_ref, v2_ref, vf_ref,
         r_out, lw_out, k_out, v_out, kk_out, b_out, g_out, carry) = refs
    else:
        (x_ref, vec_ref, win_ref, w1_ref, w2_ref, a1_ref, a2_ref, e_ref, et_ref,
         r_out, lw_out, k_out, v_out, kk_out, b_out, g_out, carry) = refs

    def vec(i):
        return vec_ref[i:i + 1, :]

    @pl.when(pl.program_id(1) == 0)
    def _():
        carry[...] = jnp.zeros_like(carry)

    xn = _rms(x_ref[...], vec(_NORM), NORM_EPS)
    tm = xn.shape[0]
    rolled = pltpu.roll(xn, 1, axis=0)
    row = lax.broadcasted_iota(jnp.int32, (tm, 1), 0)
    prev = jnp.where(row == 0, carry[7:8, :], rolled)
    carry[...] = xn[tm - 8:, :]
    xx = prev - xn

    def mix(p):
        return xn + xx * vec(_MU + p)

    r = _bdot(mix(0), win_ref[0])
    k = _bdot(mix(1), win_ref[1])
    xm_v = mix(2)
    v = _bdot(xm_v, win_ref[2])
    g_out[...] = _bdot(mix(3), win_ref[3])

    wl = vec(_W0) + _bdot(jnp.tanh(_bdot(mix(4), w1_ref[...])), w2_ref[...])
    w = -jax.nn.softplus(-wl) - 0.5
    lw_out[...] = -jnp.exp(w)
    lr = _sigmoid(vec(_A0) + _bdot(_bdot(mix(5), a1_ref[...]), a2_ref[...]))
    if has_vres:
        gate_v = _sigmoid(vec(_V0) + _bdot(_bdot(xm_v, v1_ref[...]), v2_ref[...]))
        v = v + (vf_ref[...] - v) * gate_v

    kk = k * vec(_KK)
    ss = _dot_hi_lo(kk * kk, e_ref[...])
    inv = 1.0 / jnp.maximum(jnp.sqrt(ss), 1e-12)
    kk = kk * _dot_hi_lo(inv, et_ref[...])
    r_out[...] = r
    v_out[...] = v
    kk_out[...] = kk
    b_out[...] = kk * lr
    k_out[...] = k * (1.0 + (lr - 1.0) * vec(_KA))


def _rwkv_pre(x, vecs, w_in, w1, w2, a1, a2, seg, seg_t, vres):
    bsz, seq, d = x.shape
    tok = pl.BlockSpec((None, TM, d), lambda b, j: (b, j, 0))

    def full(a):
        return pl.BlockSpec(a.shape, lambda b, j: (0,) * a.ndim)

    args = [x, vecs, w_in, w1, w2, a1, a2, seg, seg_t]
    specs = [tok] + [full(a) for a in args[1:]]
    if vres is not None:
        v1, v2, v_first = vres
        args += [v1, v2, v_first]
        specs += [full(v1), full(v2), tok]
    out = jax.ShapeDtypeStruct((bsz, seq, d), F32)
    return pl.pallas_call(
        functools.partial(_rwkv_pre_kernel, vres is not None),
        grid=(bsz, seq // TM),
        in_specs=specs,
        out_specs=[tok] * 7,
        out_shape=[out] * 7,
        scratch_shapes=[pltpu.VMEM((8, d), F32)],
        compiler_params=_params(("arbitrary", "arbitrary")),
        name="rwkv_pre",
    )(*args)


def _stack(x, lane_head):
    zero = jnp.zeros_like(x)
    return jnp.concatenate([jnp.where(lane_head == 0, x, zero),
                            jnp.where(lane_head == 1, x, zero)], axis=0).astype(BF16)


def _wkv_chunk(ops, states, tri):
    n = 2 * CHUNK
    units = range(len(ops))
    lane_head = lax.broadcasted_iota(jnp.int32, (CHUNK, LANES), 1) // A_HEAD

    def cumsum(lw):
        h1 = lw.astype(BF16)
        r1 = lw - h1.astype(F32)
        h2 = r1.astype(BF16)
        h3 = (r1 - h2.astype(F32)).astype(BF16)
        return (jnp.dot(tri, h1, preferred_element_type=F32) + jnp.dot(tri, h2, preferred_element_type=F32)
                + jnp.dot(tri, h3, preferred_element_type=F32))

    cls = [cumsum(ops[u][1]) for u in units]

    def prep(op, cl):
        r, lw, k, v, kk, b = op
        e_neg = jnp.exp(-cl)
        e_end = jnp.exp(cl[CHUNK - 1:CHUNK, :])
        bt = b * e_neg
        kt = k * e_neg
        a_s = _stack(-kk * jnp.exp(cl - lw), lane_head)
        r_s = _stack(r * jnp.exp(cl), lane_head)
        return dict(
            e_end=e_end, a_s=a_s, r_s=r_s, v_s=_stack(v, lane_head),
            ar=jnp.concatenate([a_s, r_s], axis=0),
            bk=jnp.concatenate([_stack(bt, lane_head), _stack(kt, lane_head)], axis=0),
            bkh=jnp.concatenate([_stack(bt * e_end, lane_head), _stack(kt * e_end, lane_head)], axis=0))

    pre = [prep(ops[u], cls[u]) for u in units]
    sc = [_nt(pre[u]["ar"], pre[u]["bk"]) for u in units]

    row = lax.broadcasted_iota(jnp.int32, (n, n), 0) % CHUNK
    col = lax.broadcasted_iota(jnp.int32, (n, n), 1) % CHUNK
    strict = col < row
    incl = col <= row
    eye = (lax.broadcasted_iota(jnp.int32, (n, n), 0)
           == lax.broadcasted_iota(jnp.int32, (n, n), 1)).astype(F32)
    a_ab = [jnp.where(strict, sc[u][:n, :n], 0.0) for u in units]
    a_ak = [jnp.where(strict, sc[u][:n, n:], 0.0).astype(BF16) for u in units]
    a_r = [jnp.concatenate([jnp.where(incl, sc[u][n:, :n], 0.0), jnp.where(incl, sc[u][n:, n:], 0.0)],
                           axis=1).astype(BF16) for u in units]

    t_inv = [eye + a_ab[u] for u in units]
    pw = a_ab
    for _ in range(int(math.log2(CHUNK)) - 1):
        pw = [_bdot(pw[u], pw[u]) for u in units]
        t_inv = [t_inv[u] + _bdot(t_inv[u], pw[u]) for u in units]

    akv = [_bdot(a_ak[u], pre[u]["v_s"]) for u in units]
    st = [states[u].astype(BF16) for u in units]
    rhs = [_nt(pre[u]["a_s"], st[u]) + akv[u] for u in units]
    uv = [jnp.concatenate([_bdot(t_inv[u], rhs[u]).astype(BF16), pre[u]["v_s"]], axis=0) for u in units]
    new_states = [states[u] * pre[u]["e_end"] + _tn(uv[u], pre[u]["bkh"]) for u in units]
    ys = [_nt(pre[u]["r_s"], st[u]) + _bdot(a_r[u], uv[u]) for u in units]
    return [y[:CHUNK] + y[CHUNK:] for y in ys], new_states


def _wkv_kernel(r_ref, lw_ref, k_ref, v_ref, kk_ref, b_ref, y_ref, state):
    @pl.when(pl.program_id(0) == 0)
    def _():
        state[...] = jnp.zeros_like(state)

    tri = (lax.broadcasted_iota(jnp.int32, (CHUNK, CHUNK), 1)
           <= lax.broadcasted_iota(jnp.int32, (CHUNK, CHUNK), 0)).astype(BF16)
    bsz, _, d = r_ref.shape
    where = [(bi, slice(p * LANES, (p + 1) * LANES)) for bi in range(bsz) for p in range(d // LANES)]
    ops = [tuple(ref[bi, :, sl] for ref in (r_ref, lw_ref, k_ref, v_ref, kk_ref, b_ref))
           for bi, sl in where]
    ys, new_states = _wkv_chunk(ops, [state[u] for u in range(len(where))], tri)
    for u, (bi, sl) in enumerate(where):
        y_ref[bi, :, sl] = ys[u]
        state[u] = new_states[u]


def _wkv(r, lw, k, v, kk, b):
    bsz, seq, d = r.shape
    tok = pl.BlockSpec((bsz, CHUNK, d), lambda j: (0, j, 0))
    return pl.pallas_call(
        _wkv_kernel,
        grid=(seq // CHUNK,),
        in_specs=[tok] * 6,
        out_specs=tok,
        out_shape=jax.ShapeDtypeStruct((bsz, seq, d), F32),
        scratch_shapes=[pltpu.VMEM((bsz * d // LANES, LANES, LANES), F32)],
        compiler_params=_params(("arbitrary",)),
        name="wkv",
    )(r, lw, k, v, kk, b)


def _rwkv_post_kernel(y_ref, r_ref, k_ref, v_ref, g_ref, x_ref, vec_ref, wout_ref, e_ref, et_ref,
                      out_ref):
    def vec(i):
        return vec_ref[i:i + 1, :]

    seg = e_ref[...]
    seg_t = et_ref[...]

    def head_sum(t):
        return _dot_hi_lo(_dot_hi_lo(t, seg), seg_t)

    y = y_ref[...]
    yc = y - head_sum(y) * (1.0 / A_HEAD)
    var = head_sum(yc * yc) * (1.0 / A_HEAD)
    yn = yc * lax.rsqrt(var + GN_EPS) * vec(_LNW) + vec(_LNB)
    v = v_ref[...]
    bonus = head_sum(r_ref[...] * k_ref[...] * vec(_RK)) * v
    g = g_ref[...]
    o = (yn + bonus) * (g * _sigmoid(g))
    out_ref[...] = x_ref[...] + _bdot(o, wout_ref[...])


def _rwkv_post(y, r, k, v, g, x, vecs, w_out, seg, seg_t):
    t, d = x.shape
    tok = pl.BlockSpec((TM, d), lambda i: (i, 0))

    def full(a):
        return pl.BlockSpec(a.shape, lambda i: (0,) * a.ndim)

    return pl.pallas_call(
        _rwkv_post_kernel,
        grid=(t // TM,),
        in_specs=[tok] * 6 + [full(vecs), full(w_out), full(seg), full(seg_t)],
        out_specs=tok,
        out_shape=jax.ShapeDtypeStruct((t, d), F32),
        compiler_params=_params(("arbitrary",)),
        name="rwkv_post",
    )(y, r, k, v, g, x, vecs, w_out, seg, seg_t)


def _proj_rope_kernel(scale, rot_t, lin_t, x_ref, g_ref, w_ref, pos_ref, freq_ref, rot_ref, lin_ref):
    xn = _rms(x_ref[...], g_ref[...], NORM_EPS)
    proj = _bdot(xn, w_ref[...])
    d = x_ref.shape[1]
    ang = pos_ref[...] * freq_ref[0:1, :]
    cos = jnp.cos(ang) * scale
    sin = jnp.sin(ang) * scale
    sin_lo = sin * freq_ref[1:2, :]
    sin_hi = sin * freq_ref[2:3, :]
    half = ROT_DIM // 2
    for c in range(d // LANES):
        cols = slice(c * LANES, (c + 1) * LANES)
        t = proj[:, cols]
        rot = (t * cos + pltpu.roll(t, LANES - half, axis=1) * sin_lo
               + pltpu.roll(t, half, axis=1) * sin_hi)
        if rot_t:
            rot_ref[cols, :] = rot.T.astype(rot_ref.dtype)
        else:
            rot_ref[:, cols] = rot.astype(rot_ref.dtype)
        if lin_t:
            lin_ref[cols, :] = proj[:, d + c * LANES:d + (c + 1) * LANES].T.astype(lin_ref.dtype)
    if not lin_t:
        lin_ref[...] = proj[:, d:].astype(lin_ref.dtype)


def _proj_rope(x, g, w, pos, freq, scale, rot_t, lin_t, lin_dtype):
    t, d = x.shape
    tok = pl.BlockSpec((TM, d), lambda i: (i, 0))
    tok_t = pl.BlockSpec((d, TM), lambda i: (0, i))

    def full(a):
        return pl.BlockSpec(a.shape, lambda i: (0,) * a.ndim)

    def shape(transposed, dtype):
        return jax.ShapeDtypeStruct((d, t) if transposed else (t, d), dtype)

    return pl.pallas_call(
        functools.partial(_proj_rope_kernel, scale, rot_t, lin_t),
        grid=(t // TM,),
        in_specs=[tok, full(g), full(w), pl.BlockSpec((TM, LANES), lambda i: (i, 0)), full(freq)],
        out_specs=[tok_t if rot_t else tok, tok_t if lin_t else tok],
        out_shape=[shape(rot_t, BF16), shape(lin_t, lin_dtype)],
        compiler_params=_params(("arbitrary",)),
        name="proj_rope",
    )(x, g, w, pos, freq)


def _diff_attn_kernel(lam_init, qt_ref, k_ref, vt_ref, lam_ref, sub_ref, out_ref, s_scr, acc_scr):
    i = pl.program_id(2)
    qt = qt_ref[...]
    row_map = lax.broadcasted_iota(jnp.int32, qt.shape, 0) // B_QK
    zero = jnp.zeros_like(qt)
    qs = jnp.concatenate([jnp.where(row_map == 0, qt, zero), jnp.where(row_map == 1, qt, zero)], axis=1)

    def scores(slot, j):
        kb = k_ref[pl.ds(pl.multiple_of(j * BK, BK), BK), :]
        s = jnp.dot(kb, qs, preferred_element_type=F32)
        s_scr[slot] = s
        return jnp.max(s, axis=0, keepdims=True)

    def update(slot, j, s_max, m_prev, l_prev, diagonal=False):
        s = s_scr[slot]
        if diagonal:
            kpos = lax.broadcasted_iota(jnp.int32, s.shape, 0)
            qpos = lax.broadcasted_iota(jnp.int32, s.shape, 1) % BQ
            s = jnp.where(kpos <= qpos, s, -jnp.inf)
            s_max = jnp.max(s, axis=0, keepdims=True)
        m_new = jnp.maximum(m_prev, s_max)
        p = jnp.exp2(s - m_new)
        alpha = jnp.exp2(m_prev - m_new)
        l_new = alpha * l_prev + jnp.sum(p, axis=0, keepdims=True)
        vb = vt_ref[:, pl.ds(pl.multiple_of(j * BK, BK), BK)]
        acc_scr[...] = alpha * acc_scr[...] + jnp.dot(vb, p.astype(BF16), preferred_element_type=F32)
        return m_new, l_new

    acc_scr[...] = jnp.zeros_like(acc_scr)
    m0 = jnp.full((1, 2 * BQ), -jnp.inf, F32)
    l0 = jnp.zeros((1, 2 * BQ), F32)

    def body(t, carry):
        max0, m, l = carry
        max1 = scores(1, 2 * t + 1)
        m, l = update(0, 2 * t, max0, m, l)
        max0 = scores(0, 2 * t + 2)
        m, l = update(1, 2 * t + 1, max1, m, l)
        return max0, m, l

    carry = lax.fori_loop(0, i // 2, body, (scores(0, 0), m0, l0))

    def odd_tail(carry):
        max0, m, l = carry
        scores(1, i)
        m, l = update(0, i - 1, max0, m, l)
        return update(1, i, None, m, l, diagonal=True)

    def even_tail(carry):
        _, m, l = carry
        return update(0, i, None, m, l, diagonal=True)

    m, l = lax.cond(i % 2 == 1, odd_tail, even_tail, carry)

    o_all = acc_scr[...] / l
    lam = (jnp.exp(jnp.sum(lam_ref[0:1, :] * lam_ref[1:2, :], axis=1, keepdims=True))
           - jnp.exp(jnp.sum(lam_ref[2:3, :] * lam_ref[3:4, :], axis=1, keepdims=True)) + lam_init)
    o = (o_all[:, :BQ] - lam * o_all[:, BQ:]).T
    out_ref[...] = _rms(o, sub_ref[...], SUBLN_EPS) * (1.0 - lam_init)


def _diff_attn(qt, k, vt, lam_vecs, subln, lam_init, bsz, seq):
    d, t = qt.shape
    nq = seq // BQ
    heads = d // LANES

    def full(a):
        return pl.BlockSpec(a.shape, lambda b, h, i: (0,) * a.ndim)

    return pl.pallas_call(
        functools.partial(_diff_attn_kernel, lam_init),
        grid=(bsz, heads, nq),
        in_specs=[pl.BlockSpec((LANES, BQ), lambda b, h, i: (h, b * nq + i)),
                  pl.BlockSpec((seq, LANES), lambda b, h, i: (b, h)),
                  pl.BlockSpec((LANES, seq), lambda b, h, i: (h, b)),
                  full(lam_vecs), full(subln)],
        out_specs=pl.BlockSpec((BQ, LANES), lambda b, h, i: (b * nq + i, h)),
        out_shape=jax.ShapeDtypeStruct((t, d), F32),
        scratch_shapes=[pltpu.VMEM((2, BK, 2 * BQ), F32), pltpu.VMEM((LANES, 2 * BQ), F32)],
        compiler_params=_params(("arbitrary", "arbitrary", "arbitrary")),
        name="diff_attn",
    )(qt, k, vt, lam_vecs, subln)


def _attn_post_kernel(final, o_ref, gate_ref, x_ref, wout_ref, fin_ref, out_ref):
    gate = gate_ref[...]
    o = o_ref[...] * (gate * _sigmoid(gate))
    out = x_ref[...] + _bdot(o, wout_ref[...])
    if final:
        out = _rms(out, fin_ref[...], NORM_EPS)
    out_ref[...] = out


def _attn_post(o, gate, x, w_out, fin, final):
    t, d = x.shape
    tok = pl.BlockSpec((TM, d), lambda i: (i, 0))

    def full(a):
        return pl.BlockSpec(a.shape, lambda i: (0,) * a.ndim)

    return pl.pallas_call(
        functools.partial(_attn_post_kernel, final),
        grid=(t // TM,),
        in_specs=[tok, tok, tok, full(w_out), full(fin)],
        out_specs=tok,
        out_shape=jax.ShapeDtypeStruct((t, d), F32),
        compiler_params=_params(("arbitrary",)),
        name="attn_post",
    )(o, gate, x, w_out, fin)


def _rope_tables():
    lane = jnp.arange(LANES) % B_QK
    half = ROT_DIM // 2
    inv_freq = ROPE_THETA ** (-jnp.arange(0, ROT_DIM, 2, dtype=F32) / ROT_DIM)
    freq = jnp.where(lane < ROT_DIM, inv_freq[lane % half], 0.0)
    lo = jnp.where(lane < half, -1.0, 0.0)
    hi = jnp.where((lane >= half) & (lane < ROT_DIM), 1.0, 0.0)
    return jnp.zeros((8, LANES), F32).at[0].set(freq).at[1].set(lo).at[2].set(hi)


def kernel(x, positions, a_norm, a_mu, a_w_in, a_w0, a_w1, a_w2, a_a0, a_a1, a_a2, a_v0, a_v1, a_v2, a_k_k, a_k_a, a_r_k, a_ln_w, a_ln_b, a_w_out, kv_norm, w_kv, b_norm, b_w_in, b_lq1, b_lk1, b_lq2, b_lk2, b_subln, b_w_out, final_norm):
    bsz, seq, d = x.shape
    t = bsz * seq
    seg = (jnp.arange(d)[:, None] // A_HEAD == jnp.arange(LANES)[None, :]).astype(BF16)
    seg_t = seg.T

    v_first = None
    for layer in range(N_A):
        zero = jnp.zeros((d,), F32)
        v0 = a_v0[layer - 1] if layer > 0 else zero
        vecs = jnp.stack([*a_mu[layer], a_norm[layer], a_w0[layer], a_a0[layer], v0, a_k_k[layer],
                          a_k_a[layer], a_r_k[layer].reshape(d), a_ln_w[layer], a_ln_b[layer], zero])
        vres = None
        if layer > 0:
            vres = (a_v1[layer - 1].astype(BF16), a_v2[layer - 1].astype(BF16), v_first)
        r, lw, k, v, kk, b, g = _rwkv_pre(
            x, vecs, a_w_in[layer].astype(BF16), a_w1[layer].astype(BF16), a_w2[layer].astype(BF16),
            a_a1[layer].astype(BF16), a_a2[layer].astype(BF16), seg, seg_t, vres)
        if layer == 0:
            v_first = v
        y = _wkv(r, lw, k, v, kk, b)
        flat = lambda a: a.reshape(t, d)
        x = _rwkv_post(flat(y), flat(r), flat(k), flat(v), flat(g), flat(x), vecs,
                       a_w_out[layer].astype(BF16), seg, seg_t).reshape(bsz, seq, d)

    x = x.reshape(t, d)
    pos = jnp.broadcast_to(positions.reshape(t, 1).astype(F32), (t, LANES))
    freq = _rope_tables()
    k_sh, vt_sh = _proj_rope(x, kv_norm.reshape(1, d), w_kv.astype(BF16), pos, freq, 1.0,
                             False, True, BF16)
    n_b = b_norm.shape[0]
    for j in range(n_b):
        layer = N_A + j
        lam_init = 0.8 - 0.6 * math.exp(-0.3 * layer)
        qt, gate = _proj_rope(x, b_norm[j].reshape(1, d), b_w_in[j].astype(BF16), pos, freq,
                              B_QK ** -0.5 * math.log2(math.e), True, False, F32)
        lam_vecs = jnp.zeros((8, LANES), F32).at[:4, :B_QK].set(
            jnp.stack([b_lq1[j], b_lk1[j], b_lq2[j], b_lk2[j]]))
        o = _diff_attn(qt, k_sh, vt_sh, lam_vecs, b_subln[j].reshape(1, B_V), lam_init, bsz, seq)
        x = _attn_post(o, gate, x, b_w_out[j].astype(BF16), final_norm.reshape(1, d), j == n_b - 1)
    return x.reshape(bsz, seq, d)
```

```python
import functools
import math

import jax
import jax.numpy as jnp
from jax import lax
from jax.experimental import pallas as pl
from jax.experimental.pallas import tpu as pltpu

F32 = jnp.float32
BF16 = jnp.bfloat16

A_HEAD = 64
B_QK = 64
B_V = 128
ROT_DIM = B_QK // 4
ROPE_THETA = 500000.0
NORM_EPS = 1e-6
SUBLN_EPS = 1e-5
GN_EPS = 64e-5
N_A = 2

LANES = 128
VMEM_LIMIT = 48 * 1024 * 1024

CHUNK = 64
WKV_CHUNKS = 2
TM = 256
BQ = 512
BK = 512

_MU, _NORM, _W0, _A0, _V0, _KK, _KA, _RK, _LNW, _LNB = 0, 6, 7, 8, 9, 10, 11, 12, 13, 14


def _params(sem):
    return pltpu.CompilerParams(dimension_semantics=sem, vmem_limit_bytes=VMEM_LIMIT)


def _bdot(a, b):
    return jnp.dot(a.astype(BF16), b.astype(BF16), preferred_element_type=F32)


def _nt(a, b):
    return lax.dot_general(a.astype(BF16), b.astype(BF16), (((1,), (1,)), ((), ())),
                           preferred_element_type=F32)


def _tn(a, b):
    return lax.dot_general(a.astype(BF16), b.astype(BF16), (((0,), (0,)), ((), ())),
                           preferred_element_type=F32)


def _dot_hi_lo(x, m):
    hi = x.astype(BF16)
    lo = (x - hi.astype(F32)).astype(BF16)
    return (jnp.dot(hi, m, preferred_element_type=F32) + jnp.dot(lo, m, preferred_element_type=F32))


def _rms(x, g, eps):
    return x * lax.rsqrt(jnp.mean(x * x, axis=-1, keepdims=True) + eps) * g


def _sigmoid(x):
    return 1.0 / (1.0 + jnp.exp(-x))


def _rwkv_pre_kernel(has_vres, *refs):
    if has_vres:
        (x_ref, vec_ref, win_ref, w1_ref, w2_ref, a1_ref, a2_ref, e_ref, et_ref,
         v1_ref, v2_ref, vf_ref,
         r_out, lw_out, k_out, v_out, kk_out, b_out, g_out, carry) = refs
    else:
        (x_ref, vec_ref, win_ref, w1_ref, w2_ref, a1_ref, a2_ref, e_ref, et_ref,
         r_out, lw_out, k_out, v_out, kk_out, b_out, g_out, carry) = refs

    def vec(i):
        return vec_ref[i:i + 1, :]

    @pl.when(pl.program_id(1) == 0)
    def _():
        carry[...] = jnp.zeros_like(carry)

    xn = _rms(x_ref[...], vec(_NORM), NORM_EPS)
    tm = xn.shape[0]
    rolled = pltpu.roll(xn, 1, axis=0)
    row = lax.broadcasted_iota(jnp.int32, (tm, 1), 0)
    prev = jnp.where(row == 0, carry[7:8, :], rolled)
    carry[...] = xn[tm - 8:, :]
    xx = prev - xn

    def mix(p):
        return xn + xx * vec(_MU + p)

    r = _bdot(mix(0), win_ref[0])
    k = _bdot(mix(1), win_ref[1])
    xm_v = mix(2)
    v = _bdot(xm_v, win_ref[2])
    g_out[...] = _bdot(mix(3), win_ref[3])

    wl = vec(_W0) + _bdot(jnp.tanh(_bdot(mix(4), w1_ref[...])), w2_ref[...])
    w = -jax.nn.softplus(-wl) - 0.5
    lw_out[...] = -jnp.exp(w)
    lr = _sigmoid(vec(_A0) + _bdot(_bdot(mix(5), a1_ref[...]), a2_ref[...]))
    if has_vres:
        gate_v = _sigmoid(vec(_V0) + _bdot(_bdot(xm_v, v1_ref[...]), v2_ref[...]))
        v = v + (vf_ref[...] - v) * gate_v

    kk = k * vec(_KK)
    ss = _dot_hi_lo(kk * kk, e_ref[...])
    inv = 1.0 / jnp.maximum(jnp.sqrt(ss), 1e-12)
    kk = kk * _dot_hi_lo(inv, et_ref[...])
    r_out[...] = r
    v_out[...] = v
    kk_out[...] = kk
    b_out[...] = kk * lr
    k_out[...] = k * (1.0 + (lr - 1.0) * vec(_KA))


def _rwkv_pre(x, vecs, w_in, w1, w2, a1, a2, seg, seg_t, vres):
    bsz, seq, d = x.shape
    tok = pl.BlockSpec((None, TM, d), lambda b, j: (b, j, 0))

    def full(a):
        return pl.BlockSpec(a.shape, lambda b, j: (0,) * a.ndim)

    args = [x, vecs, w_in, w1, w2, a1, a2, seg, seg_t]
    specs = [tok] + [full(a) for a in args[1:]]
    if vres is not None:
        v1, v2, v_first = vres
        args += [v1, v2, v_first]
        specs += [full(v1), full(v2), tok]
    out = jax.ShapeDtypeStruct((bsz, seq, d), F32)
    return pl.pallas_call(
        functools.partial(_rwkv_pre_kernel, vres is not None),
        grid=(bsz, seq // TM),
        in_specs=specs,
        out_specs=[tok] * 7,
        out_shape=[out] * 7,
        scratch_shapes=[pltpu.VMEM((8, d), F32)],
        compiler_params=_params(("arbitrary", "arbitrary")),
        name="rwkv_pre",
    )(*args)


def _stack(x, lane_head):
    x = x.astype(BF16)
    zero = jnp.zeros_like(x)
    return jnp.concatenate([jnp.where(lane_head == 0, x, zero),
                            jnp.where(lane_head == 1, x, zero)], axis=0)


def _wkv_prepare(ops, tri):
    units = range(len(ops))
    lane = lax.broadcasted_iota(jnp.int32, (CHUNK, LANES), 1)
    lane_head = lane // A_HEAD

    def cumsum(lw):
        h1 = lw.astype(BF16)
        r1 = lw - h1.astype(F32)
        h2 = r1.astype(BF16)
        h3 = (r1 - h2.astype(F32)).astype(BF16)
        return (jnp.dot(tri, h1, preferred_element_type=F32) + jnp.dot(tri, h2, preferred_element_type=F32)
                + jnp.dot(tri, h3, preferred_element_type=F32))

    cls = [cumsum(ops[u][1]) for u in units]

    def prep(op, cl):
        r, lw, k, v, kk, b = op
        e_neg = jnp.exp(-cl)
        e_end = jnp.exp(cl[CHUNK - 1:CHUNK, :])
        bt = b * e_neg
        kt = k * e_neg
        a_n = (-kk * jnp.exp(cl - lw)).astype(BF16)
        r_n = (r * jnp.exp(cl)).astype(BF16)
        return dict(
            e_end=e_end, a_n=a_n, r_n=r_n, v_n=v.astype(BF16), v_s=_stack(v, lane_head),
            ar=jnp.concatenate([a_n, r_n], axis=0),
            bk=jnp.concatenate([_stack(bt, lane_head), _stack(kt, lane_head)], axis=0),
            bkh=jnp.concatenate([bt * e_end, kt * e_end], axis=0).astype(BF16))

    pre = [prep(ops[u], cls[u]) for u in units]
    sc = [_nt(pre[u]["ar"], pre[u]["bk"]) for u in units]

    row = lax.broadcasted_iota(jnp.int32, (CHUNK, LANES), 0)
    col = lane % A_HEAD
    strict = col < row
    incl = col <= row
    eye = (col == row).astype(F32)
    a_ab = [jnp.where(strict, sc[u][:CHUNK, :LANES], 0.0) for u in units]
    a_ak = [jnp.where(strict, sc[u][:CHUNK, LANES:], 0.0).astype(BF16) for u in units]
    a_r = [jnp.concatenate([jnp.where(incl, sc[u][CHUNK:, :LANES], 0.0),
                            jnp.where(incl, sc[u][CHUNK:, LANES:], 0.0)], axis=1).astype(BF16)
           for u in units]

    t_inv = [eye + a_ab[u] for u in units]
    pw = [_bdot(a_ab[u], _stack(a_ab[u], lane_head)) for u in units]
    for _ in range(int(math.log2(CHUNK)) - 2):
        both = [_bdot(jnp.concatenate([t_inv[u], pw[u]], axis=0), _stack(pw[u], lane_head))
                for u in units]
        t_inv = [t_inv[u] + both[u][:CHUNK] for u in units]
        pw = [both[u][CHUNK:] for u in units]
    t_inv = [t_inv[u] + _bdot(t_inv[u], _stack(pw[u], lane_head)) for u in units]

    akv = [_bdot(a_ak[u], pre[u]["v_s"]) for u in units]
    return [dict(pre[u], t_inv=t_inv[u].astype(BF16), akv=akv[u], a_r=a_r[u]) for u in units]


def _wkv_advance(prep, states):
    units = range(len(prep))
    lane_head = lax.broadcasted_iota(jnp.int32, (CHUNK, LANES), 1) // A_HEAD
    st = [states[u].astype(BF16) for u in units]
    ar_st = [_nt(prep[u]["ar"], st[u]) for u in units]
    rhs = [ar_st[u][:CHUNK] + prep[u]["akv"] for u in units]
    us = [jnp.dot(prep[u]["t_inv"], _stack(rhs[u], lane_head), preferred_element_type=F32)
          for u in units]
    uv_n = [jnp.concatenate([us[u].astype(BF16), prep[u]["v_n"]], axis=0) for u in units]
    same_head = (lax.broadcasted_iota(jnp.int32, (LANES, LANES), 0) // A_HEAD
                 == lax.broadcasted_iota(jnp.int32, (LANES, LANES), 1) // A_HEAD)
    new_states = [states[u] * prep[u]["e_end"] + jnp.where(same_head, _tn(uv_n[u], prep[u]["bkh"]), 0.0)
                  for u in units]
    ys = [ar_st[u][CHUNK:]
          + _bdot(prep[u]["a_r"], jnp.concatenate([_stack(us[u], lane_head), prep[u]["v_s"]], axis=0))
          for u in units]
    return ys, new_states


def _wkv_kernel(r_ref, lw_ref, k_ref, v_ref, kk_ref, b_ref, y_ref, state):
    @pl.when(pl.program_id(0) == 0)
    def _():
        state[...] = jnp.zeros_like(state)

    tri = (lax.broadcasted_iota(jnp.int32, (CHUNK, CHUNK), 1)
           <= lax.broadcasted_iota(jnp.int32, (CHUNK, CHUNK), 0)).astype(BF16)
    bsz, _, d = r_ref.shape
    pairs = [(bi, slice(p * LANES, (p + 1) * LANES)) for bi in range(bsz) for p in range(d // LANES)]
    rows = [slice(c * CHUNK, (c + 1) * CHUNK) for c in range(WKV_CHUNKS)]
    prep = _wkv_prepare([tuple(ref[bi, rs, sl] for ref in (r_ref, lw_ref, k_ref, v_ref, kk_ref, b_ref))
                         for rs in rows for bi, sl in pairs], tri)
    states = [state[u] for u in range(len(pairs))]
    for c, rs in enumerate(rows):
        ys, states = _wkv_advance(prep[c * len(pairs):(c + 1) * len(pairs)], states)
        for u, (bi, sl) in enumerate(pairs):
            y_ref[bi, rs, sl] = ys[u]
    for u in range(len(pairs)):
        state[u] = states[u]


def _wkv(r, lw, k, v, kk, b):
    bsz, seq, d = r.shape
    tok = pl.BlockSpec((bsz, WKV_CHUNKS * CHUNK, d), lambda j: (0, j, 0))
    return pl.pallas_call(
        _wkv_kernel,
        grid=(seq // (WKV_CHUNKS * CHUNK),),
        in_specs=[tok] * 6,
        out_specs=tok,
        out_shape=jax.ShapeDtypeStruct((bsz, seq, d), F32),
        scratch_shapes=[pltpu.VMEM((bsz * d // LANES, LANES, LANES), F32)],
        compiler_params=_params(("arbitrary",)),
        name="wkv",
    )(r, lw, k, v, kk, b)


def _rwkv_post_kernel(y_ref, r_ref, k_ref, v_ref, g_ref, x_ref, vec_ref, wout_ref, e_ref, et_ref,
                      out_ref):
    def vec(i):
        return vec_ref[i:i + 1, :]

    seg = e_ref[...]
    seg_t = et_ref[...]

    def head_sum(t):
        return _dot_hi_lo(_dot_hi_lo(t, seg), seg_t)

    y = y_ref[...]
    yc = y - head_sum(y) * (1.0 / A_HEAD)
    var = head_sum(yc * yc) * (1.0 / A_HEAD)
    yn = yc * lax.rsqrt(var + GN_EPS) * vec(_LNW) + vec(_LNB)
    v = v_ref[...]
    bonus = head_sum(r_ref[...] * k_ref[...] * vec(_RK)) * v
    g = g_ref[...]
    o = (yn + bonus) * (g * _sigmoid(g))
    out_ref[...] = x_ref[...] + _bdot(o, wout_ref[...])


def _rwkv_post(y, r, k, v, g, x, vecs, w_out, seg, seg_t):
    t, d = x.shape
    tok = pl.BlockSpec((TM, d), lambda i: (i, 0))

    def full(a):
        return pl.BlockSpec(a.shape, lambda i: (0,) * a.ndim)

    return pl.pallas_call(
        _rwkv_post_kernel,
        grid=(t // TM,),
        in_specs=[tok] * 6 + [full(vecs), full(w_out), full(seg), full(seg_t)],
        out_specs=tok,
        out_shape=jax.ShapeDtypeStruct((t, d), F32),
        compiler_params=_params(("arbitrary",)),
        name="rwkv_post",
    )(y, r, k, v, g, x, vecs, w_out, seg, seg_t)


def _proj_rope_kernel(scale, rot_t, lin_t, x_ref, g_ref, w_ref, pos_ref, freq_ref, rot_ref, lin_ref):
    xn = _rms(x_ref[...], g_ref[...], NORM_EPS)
    proj = _bdot(xn, w_ref[...])
    d = x_ref.shape[1]
    ang = pos_ref[...] * freq_ref[0:1, :]
    cos = jnp.cos(ang) * scale
    sin = jnp.sin(ang) * scale
    sin_lo = sin * freq_ref[1:2, :]
    sin_hi = sin * freq_ref[2:3, :]
    half = ROT_DIM // 2
    for c in range(d // LANES):
        cols = slice(c * LANES, (c + 1) * LANES)
        t = proj[:, cols]
        rot = (t * cos + pltpu.roll(t, LANES - half, axis=1) * sin_lo
               + pltpu.roll(t, half, axis=1) * sin_hi)
        if rot_t:
            rot_ref[cols, :] = rot.T.astype(rot_ref.dtype)
        else:
            rot_ref[:, cols] = rot.astype(rot_ref.dtype)
        if lin_t:
            lin_ref[cols, :] = proj[:, d + c * LANES:d + (c + 1) * LANES].T.astype(lin_ref.dtype)
    if not lin_t:
        lin_ref[...] = proj[:, d:].astype(lin_ref.dtype)


def _proj_rope(x, g, w, pos, freq, scale, rot_t, lin_t, lin_dtype):
    t, d = x.shape
    tok = pl.BlockSpec((TM, d), lambda i: (i, 0))
    tok_t = pl.BlockSpec((d, TM), lambda i: (0, i))

    def full(a):
        return pl.BlockSpec(a.shape, lambda i: (0,) * a.ndim)

    def shape(transposed, dtype):
        return jax.ShapeDtypeStruct((d, t) if transposed else (t, d), dtype)

    return pl.pallas_call(
        functools.partial(_proj_rope_kernel, scale, rot_t, lin_t),
        grid=(t // TM,),
        in_specs=[tok, full(g), full(w), pl.BlockSpec((TM, LANES), lambda i: (i, 0)), full(freq)],
        out_specs=[tok_t if rot_t else tok, tok_t if lin_t else tok],
        out_shape=[shape(rot_t, BF16), shape(lin_t, lin_dtype)],
        compiler_params=_params(("arbitrary",)),
        name="proj_rope",
    )(x, g, w, pos, freq)


def _diff_attn_kernel(lam_init, qt_ref, k_ref, vt_ref, lam_ref, sub_ref, out_ref, s_scr, acc_scr):
    i = pl.program_id(2)
    qt = qt_ref[...]
    row_map = lax.broadcasted_iota(jnp.int32, qt.shape, 0) // B_QK
    zero = jnp.zeros_like(qt)
    qs = jnp.concatenate([jnp.where(row_map == 0, qt, zero), jnp.where(row_map == 1, qt, zero)], axis=1)

    def scores(slot, j):
        kb = k_ref[pl.ds(pl.multiple_of(j * BK, BK), BK), :]
        s = jnp.dot(kb, qs, preferred_element_type=F32)
        s_scr[slot] = s
        return jnp.max(s, axis=0, keepdims=True)

    def update(slot, j, s_max, m_prev, l_prev, diagonal=False):
        s = s_scr[slot]
        if diagonal:
            kpos = lax.broadcasted_iota(jnp.int32, s.shape, 0)
            qpos = lax.broadcasted_iota(jnp.int32, s.shape, 1) % BQ
            s = jnp.where(kpos <= qpos, s, -jnp.inf)
            s_max = jnp.max(s, axis=0, keepdims=True)
        m_new = jnp.maximum(m_prev, s_max)
        p = jnp.exp2(s - m_new)
        alpha = jnp.exp2(m_prev - m_new)
        l_new = alpha * l_prev + jnp.sum(p, axis=0, keepdims=True)
        vb = vt_ref[:, pl.ds(pl.multiple_of(j * BK, BK), BK)]
        acc_scr[...] = alpha * acc_scr[...] + jnp.dot(vb, p.astype(BF16), preferred_element_type=F32)
        return m_new, l_new

    acc_scr[...] = jnp.zeros_like(acc_scr)
    m0 = jnp.full((1, 2 * BQ), -jnp.inf, F32)
    l0 = jnp.zeros((1, 2 * BQ), F32)

    def pair(j, carry):
        max0, m, l = carry
        max1 = scores(1, j + 1)
        m, l = update(0, j, max0, m, l)
        max0 = scores(0, j + 2)
        m, l = update(1, j + 1, max1, m, l)
        return max0, m, l

    carry = (scores(0, 0), m0, l0)
    carry = lax.fori_loop(0, i // 4, lambda t, c: pair(4 * t + 2, pair(4 * t, c)), carry)
    carry = lax.fori_loop(0, (i % 4) // 2, lambda t, c: pair(i - i % 4, c), carry)

    def odd_tail(carry):
        max0, m, l = carry
        scores(1, i)
        m, l = update(0, i - 1, max0, m, l)
        return update(1, i, None, m, l, diagonal=True)

    def even_tail(carry):
        _, m, l = carry
        return update(0, i, None, m, l, diagonal=True)

    m, l = lax.cond(i % 2 == 1, odd_tail, even_tail, carry)

    o_all = acc_scr[...] / l
    lam = (jnp.exp(jnp.sum(lam_ref[0:1, :] * lam_ref[1:2, :], axis=1, keepdims=True))
           - jnp.exp(jnp.sum(lam_ref[2:3, :] * lam_ref[3:4, :], axis=1, keepdims=True)) + lam_init)
    o = (o_all[:, :BQ] - lam * o_all[:, BQ:]).T
    out_ref[...] = _rms(o, sub_ref[...], SUBLN_EPS) * (1.0 - lam_init)


def _diff_attn(qt, k, vt, lam_vecs, subln, lam_init, bsz, seq):
    d, t = qt.shape
    nq = seq // BQ
    heads = d // LANES

    def full(a):
        return pl.BlockSpec(a.shape, lambda b, h, i: (0,) * a.ndim)

    return pl.pallas_call(
        functools.partial(_diff_attn_kernel, lam_init),
        grid=(bsz, heads, nq),
        in_specs=[pl.BlockSpec((LANES, BQ), lambda b, h, i: (h, b * nq + i)),
                  pl.BlockSpec((seq, LANES), lambda b, h, i: (b, h)),
                  pl.BlockSpec((LANES, seq), lambda b, h, i: (h, b)),
                  full(lam_vecs), full(subln)],
        out_specs=pl.BlockSpec((BQ, LANES), lambda b, h, i: (b * nq + i, h)),
        out_shape=jax.ShapeDtypeStruct((t, d), F32),
        scratch_shapes=[pltpu.VMEM((2, BK, 2 * BQ), F32), pltpu.VMEM((LANES, 2 * BQ), F32)],
        compiler_params=_params(("arbitrary", "arbitrary", "arbitrary")),
        name="diff_attn",
    )(qt, k, vt, lam_vecs, subln)


def _attn_post_kernel(final, o_ref, gate_ref, x_ref, wout_ref, fin_ref, out_ref):
    gate = gate_ref[...]
    o = o_ref[...] * (gate * _sigmoid(gate))
    out = x_ref[...] + _bdot(o, wout_ref[...])
    if final:
        out = _rms(out, fin_ref[...], NORM_EPS)
    out_ref[...] = out


def _attn_post(o, gate, x, w_out, fin, final):
    t, d = x.shape
    tok = pl.BlockSpec((TM, d), lambda i: (i, 0))

    def full(a):
        return pl.BlockSpec(a.shape, lambda i: (0,) * a.ndim)

    return pl.pallas_call(
        functools.partial(_attn_post_kernel, final),
        grid=(t // TM,),
        in_specs=[tok, tok, tok, full(w_out), full(fin)],
        out_specs=tok,
        out_shape=jax.ShapeDtypeStruct((t, d), F32),
        compiler_params=_params(("arbitrary",)),
        name="attn_post",
    )(o, gate, x, w_out, fin)


def _rope_tables():
    lane = jnp.arange(LANES) % B_QK
    half = ROT_DIM // 2
    inv_freq = ROPE_THETA ** (-jnp.arange(0, ROT_DIM, 2, dtype=F32) / ROT_DIM)
    freq = jnp.where(lane < ROT_DIM, inv_freq[lane % half], 0.0)
    lo = jnp.where(lane < half, -1.0, 0.0)
    hi = jnp.where((lane >= half) & (lane < ROT_DIM), 1.0, 0.0)
    return jnp.zeros((8, LANES), F32).at[0].set(freq).at[1].set(lo).at[2].set(hi)


def kernel(x, positions, a_norm, a_mu, a_w_in, a_w0, a_w1, a_w2, a_a0, a_a1, a_a2, a_v0, a_v1, a_v2, a_k_k, a_k_a, a_r_k, a_ln_w, a_ln_b, a_w_out, kv_norm, w_kv, b_norm, b_w_in, b_lq1, b_lk1, b_lq2, b_lk2, b_subln, b_w_out, final_norm):
    bsz, seq, d = x.shape
    t = bsz * seq
    seg = (jnp.arange(d)[:, None] // A_HEAD == jnp.arange(LANES)[None, :]).astype(BF16)
    seg_t = seg.T

    v_first = None
    for layer in range(N_A):
        zero = jnp.zeros((d,), F32)
        v0 = a_v0[layer - 1] if layer > 0 else zero
        vecs = jnp.stack([*a_mu[layer], a_norm[layer], a_w0[layer], a_a0[layer], v0, a_k_k[layer],
                          a_k_a[layer], a_r_k[layer].reshape(d), a_ln_w[layer], a_ln_b[layer], zero])
        vres = None
        if layer > 0:
            vres = (a_v1[layer - 1].astype(BF16), a_v2[layer - 1].astype(BF16), v_first)
        r, lw, k, v, kk, b, g = _rwkv_pre(
            x, vecs, a_w_in[layer].astype(BF16), a_w1[layer].astype(BF16), a_w2[layer].astype(BF16),
            a_a1[layer].astype(BF16), a_a2[layer].astype(BF16), seg, seg_t, vres)
        if layer == 0:
            v_first = v
        y = _wkv(r, lw, k, v, kk, b)
        flat = lambda a: a.reshape(t, d)
        x = _rwkv_post(flat(y), flat(r), flat(k), flat(v), flat(g), flat(x), vecs,
                       a_w_out[layer].astype(BF16), seg, seg_t).reshape(bsz, seq, d)

    x = x.reshape(t, d)
    pos = jnp.broadcast_to(positions.reshape(t, 1).astype(F32), (t, LANES))
    freq = _rope_tables()
    k_sh, vt_sh = _proj_rope(x, kv_norm.reshape(1, d), w_kv.astype(BF16), pos, freq, 1.0,
                             False, True, BF16)
    n_b = b_norm.shape[0]
    for j in range(n_b):
        layer = N_A + j
        lam_init = 0.8 - 0.6 * math.exp(-0.3 * layer)
        qt, gate = _proj_rope(x, b_norm[j].reshape(1, d), b_w_in[j].astype(BF16), pos, freq,
                              B_QK ** -0.5 * math.log2(math.e), True, False, F32)
        lam_vecs = jnp.zeros((8, LANES), F32).at[:4, :B_QK].set(
            jnp.stack([b_lq1[j], b_lk1[j], b_lq2[j], b_lk2[j]]))
        o = _diff_attn(qt, k_sh, vt_sh, lam_vecs, b_subln[j].reshape(1, B_V), lam_init, bsz, seq)
        x = _attn_post(o, gate, x, b_w_out[j].astype(BF16), final_norm.reshape(1, d), j == n_b - 1)
    return x.reshape(bsz, seq, d)
```

```python
import functools
import math

import jax
import jax.numpy as jnp
from jax import lax
from jax.experimental import pallas as pl
from jax.experimental.pallas import tpu as pltpu

F32 = jnp.float32
BF16 = jnp.bfloat16

A_HEAD = 64
B_QK = 64
B_V = 128
ROT_DIM = B_QK // 4
ROPE_THETA = 500000.0
NORM_EPS = 1e-6
SUBLN_EPS = 1e-5
GN_EPS = 64e-5
N_A = 2

LANES = 128
VMEM_LIMIT = 48 * 1024 * 1024

CHUNK = 64
WKV_CHUNKS = 2
TM = 256
TM_WIDE = 512
BQ = 512
BK = 512

_MU, _NORM, _W0, _A0, _V0, _KK, _KA, _RK, _LNW, _LNB = 0, 6, 7, 8, 9, 10, 11, 12, 13, 14


def _params(sem):
    return pltpu.CompilerParams(dimension_semantics=sem, vmem_limit_bytes=VMEM_LIMIT)


def _bdot(a, b):
    return jnp.dot(a.astype(BF16), b.astype(BF16), preferred_element_type=F32)


def _nt(a, b):
    return lax.dot_general(a.astype(BF16), b.astype(BF16), (((1,), (1,)), ((), ())),
                           preferred_element_type=F32)


def _tn(a, b):
    return lax.dot_general(a.astype(BF16), b.astype(BF16), (((0,), (0,)), ((), ())),
                           preferred_element_type=F32)


def _dot_hi_lo(x, m):
    hi = x.astype(BF16)
    lo = (x - hi.astype(F32)).astype(BF16)
    return (jnp.dot(hi, m, preferred_element_type=F32) + jnp.dot(lo, m, preferred_element_type=F32))


def _rms(x, g, eps):
    return x * lax.rsqrt(jnp.mean(x * x, axis=-1, keepdims=True) + eps) * g


def _sigmoid(x):
    return 1.0 / (1.0 + jnp.exp(-x))


def _rwkv_pre_kernel(has_vres, *refs):
    if has_vres:
        (x_ref, vec_ref, win_ref, w1_ref, w2_ref, a1_ref, a2_ref, e_ref, et_ref,
         v1_ref, v2_ref, vf_ref,
         r_out, lw_out, k_out, v_out, kk_out, b_out, g_out, carry) = refs
    else:
        (x_ref, vec_ref, win_ref, w1_ref, w2_ref, a1_ref, a2_ref, e_ref, et_ref,
         r_out, lw_out, k_out, v_out, kk_out, b_out, g_out, carry) = refs

    def vec(i):
        return vec_ref[i:i + 1, :]

    @pl.when(pl.program_id(1) == 0)
    def _():
        carry[...] = jnp.zeros_like(carry)

    xn = _rms(x_ref[...], vec(_NORM), NORM_EPS)
    tm = xn.shape[0]
    rolled = pltpu.roll(xn, 1, axis=0)
    row = lax.broadcasted_iota(jnp.int32, (tm, 1), 0)
    prev = jnp.where(row == 0, carry[7:8, :], rolled)
    carry[...] = xn[tm - 8:, :]
    xx = prev - xn

    def mix(p):
        return xn + xx * vec(_MU + p)

    r = _bdot(mix(0), win_ref[0])
    k = _bdot(mix(1), win_ref[1])
    xm_v = mix(2)
    v = _bdot(xm_v, win_ref[2])
    g_out[...] = _bdot(mix(3), win_ref[3]).astype(g_out.dtype)

    wl = vec(_W0) + _bdot(jnp.tanh(_bdot(mix(4), w1_ref[...])), w2_ref[...])
    w = -jax.nn.softplus(-wl) - 0.5
    lw_out[...] = -jnp.exp(w)
    lr = _sigmoid(vec(_A0) + _bdot(_bdot(mix(5), a1_ref[...]), a2_ref[...]))
    if has_vres:
        gate_v = _sigmoid(vec(_V0) + _bdot(_bdot(xm_v, v1_ref[...]), v2_ref[...]))
        v = v + (vf_ref[...].astype(F32) - v) * gate_v

    kk = k * vec(_KK)
    ss = _bdot(kk * kk, e_ref[...])
    inv = 1.0 / jnp.maximum(jnp.sqrt(ss), 1e-12)
    kk = kk * _dot_hi_lo(inv, et_ref[...])
    r_out[...] = r.astype(r_out.dtype)
    v_out[...] = v.astype(v_out.dtype)
    kk_out[...] = kk.astype(kk_out.dtype)
    b_out[...] = (kk * lr).astype(b_out.dtype)
    k_out[...] = (k * (1.0 + (lr - 1.0) * vec(_KA))).astype(k_out.dtype)


def _rwkv_pre(x, vecs, w_in, w1, w2, a1, a2, seg, seg_t, vres):
    bsz, seq, d = x.shape
    tok = pl.BlockSpec((None, TM, d), lambda b, j: (b, j, 0))

    def full(a):
        return pl.BlockSpec(a.shape, lambda b, j: (0,) * a.ndim, pipeline_mode=pl.Buffered(1))

    args = [x, vecs, w_in, w1, w2, a1, a2, seg, seg_t]
    specs = [tok] + [full(a) for a in args[1:]]
    if vres is not None:
        v1, v2, v_first = vres
        args += [v1, v2, v_first]
        specs += [full(v1), full(v2), tok]
    dtypes = [BF16, F32, BF16, BF16, BF16, BF16, BF16]
    return pl.pallas_call(
        functools.partial(_rwkv_pre_kernel, vres is not None),
        grid=(bsz, seq // TM),
        in_specs=specs,
        out_specs=[tok] * 7,
        out_shape=[jax.ShapeDtypeStruct((bsz, seq, d), dt) for dt in dtypes],
        scratch_shapes=[pltpu.VMEM((8, d), F32)],
        compiler_params=_params(("arbitrary", "arbitrary")),
        name="rwkv_pre",
    )(*args)


def _stack(x, lane_head):
    x = x.astype(BF16)
    zero = jnp.zeros_like(x)
    return jnp.concatenate([jnp.where(lane_head == 0, x, zero),
                            jnp.where(lane_head == 1, x, zero)], axis=0)


def _wkv_prepare(ops, tri):
    units = range(len(ops))
    lane = lax.broadcasted_iota(jnp.int32, (CHUNK, LANES), 1)
    lane_head = lane // A_HEAD

    def cumsum(lw):
        h1 = lw.astype(BF16)
        r1 = lw - h1.astype(F32)
        h2 = r1.astype(BF16)
        h3 = (r1 - h2.astype(F32)).astype(BF16)
        return (jnp.dot(tri, h1, preferred_element_type=F32) + jnp.dot(tri, h2, preferred_element_type=F32)
                + jnp.dot(tri, h3, preferred_element_type=F32))

    cls = [cumsum(ops[u][1]) for u in units]

    def prep(op, cl):
        r, lw, k, v, kk, b = (t.astype(F32) for t in op)
        e_neg = jnp.exp(-cl)
        e_end = jnp.exp(cl[CHUNK - 1:CHUNK, :])
        bt = b * e_neg
        kt = k * e_neg
        a_n = (-kk * jnp.exp(cl - lw)).astype(BF16)
        r_n = (r * jnp.exp(cl)).astype(BF16)
        return dict(
            e_end=e_end, a_n=a_n, r_n=r_n, v_n=v.astype(BF16), v_s=_stack(v, lane_head),
            ar=jnp.concatenate([a_n, r_n], axis=0),
            bk=jnp.concatenate([_stack(bt, lane_head), _stack(kt, lane_head)], axis=0),
            bkh=jnp.concatenate([bt * e_end, kt * e_end], axis=0).astype(BF16))

    pre = [prep(ops[u], cls[u]) for u in units]
    sc = [_nt(pre[u]["ar"], pre[u]["bk"]) for u in units]

    row = lax.broadcasted_iota(jnp.int32, (CHUNK, LANES), 0)
    col = lane % A_HEAD
    strict = col < row
    incl = col <= row
    eye = (col == row).astype(F32)
    a_ab = [jnp.where(strict, sc[u][:CHUNK, :LANES], 0.0) for u in units]
    a_ak = [jnp.where(strict, sc[u][:CHUNK, LANES:], 0.0).astype(BF16) for u in units]
    a_r = [jnp.concatenate([jnp.where(incl, sc[u][CHUNK:, :LANES], 0.0),
                            jnp.where(incl, sc[u][CHUNK:, LANES:], 0.0)], axis=1).astype(BF16)
           for u in units]

    t_inv = [eye + a_ab[u] for u in units]
    pw = [_bdot(a_ab[u], _stack(a_ab[u], lane_head)) for u in units]
    for _ in range(int(math.log2(CHUNK)) - 2):
        both = [_bdot(jnp.concatenate([t_inv[u], pw[u]], axis=0), _stack(pw[u], lane_head))
                for u in units]
        t_inv = [t_inv[u] + both[u][:CHUNK] for u in units]
        pw = [both[u][CHUNK:] for u in units]
    t_inv = [t_inv[u] + _bdot(t_inv[u], _stack(pw[u], lane_head)) for u in units]

    akv = [_bdot(a_ak[u], pre[u]["v_s"]) for u in units]
    return [dict(pre[u], t_inv=t_inv[u].astype(BF16), akv=akv[u], a_r=a_r[u]) for u in units]


def _wkv_advance(prep, states):
    units = range(len(prep))
    lane_head = lax.broadcasted_iota(jnp.int32, (CHUNK, LANES), 1) // A_HEAD
    st = [states[u].astype(BF16) for u in units]
    ar_st = [_nt(prep[u]["ar"], st[u]) for u in units]
    rhs = [ar_st[u][:CHUNK] + prep[u]["akv"] for u in units]
    us = [jnp.dot(prep[u]["t_inv"], _stack(rhs[u], lane_head), preferred_element_type=F32)
          for u in units]
    uv_n = [jnp.concatenate([us[u].astype(BF16), prep[u]["v_n"]], axis=0) for u in units]
    same_head = (lax.broadcasted_iota(jnp.int32, (LANES, LANES), 0) // A_HEAD
                 == lax.broadcasted_iota(jnp.int32, (LANES, LANES), 1) // A_HEAD)
    new_states = [states[u] * prep[u]["e_end"] + jnp.where(same_head, _tn(uv_n[u], prep[u]["bkh"]), 0.0)
                  for u in units]
    ys = [ar_st[u][CHUNK:]
          + _bdot(prep[u]["a_r"], jnp.concatenate([_stack(us[u], lane_head), prep[u]["v_s"]], axis=0))
          for u in units]
    return ys, new_states


def _wkv_kernel(r_ref, lw_ref, k_ref, v_ref, kk_ref, b_ref, y_ref, state):
    @pl.when(pl.program_id(0) == 0)
    def _():
        state[...] = jnp.zeros_like(state)

    tri = (lax.broadcasted_iota(jnp.int32, (CHUNK, CHUNK), 1)
           <= lax.broadcasted_iota(jnp.int32, (CHUNK, CHUNK), 0)).astype(BF16)
    bsz, _, d = r_ref.shape
    pairs = [(bi, slice(p * LANES, (p + 1) * LANES)) for bi in range(bsz) for p in range(d // LANES)]
    rows = [slice(c * CHUNK, (c + 1) * CHUNK) for c in range(WKV_CHUNKS)]
    prep = _wkv_prepare([tuple(ref[bi, rs, sl] for ref in (r_ref, lw_ref, k_ref, v_ref, kk_ref, b_ref))
                         for rs in rows for bi, sl in pairs], tri)
    states = [state[u] for u in range(len(pairs))]
    for c, rs in enumerate(rows):
        ys, states = _wkv_advance(prep[c * len(pairs):(c + 1) * len(pairs)], states)
        for u, (bi, sl) in enumerate(pairs):
            y_ref[bi, rs, sl] = ys[u].astype(y_ref.dtype)
    for u in range(len(pairs)):
        state[u] = states[u]


def _wkv(r, lw, k, v, kk, b):
    bsz, seq, d = r.shape
    tok = pl.BlockSpec((bsz, WKV_CHUNKS * CHUNK, d), lambda j: (0, j, 0))
    return pl.pallas_call(
        _wkv_kernel,
        grid=(seq // (WKV_CHUNKS * CHUNK),),
        in_specs=[tok] * 6,
        out_specs=tok,
        out_shape=jax.ShapeDtypeStruct((bsz, seq, d), BF16),
        scratch_shapes=[pltpu.VMEM((bsz * d // LANES, LANES, LANES), F32)],
        compiler_params=_params(("arbitrary",)),
        name="wkv",
    )(r, lw, k, v, kk, b)


def _rwkv_post_kernel(y_ref, r_ref, k_ref, v_ref, g_ref, x_ref, vec_ref, wout_ref, e_ref, et_ref,
                      out_ref):
    def vec(i):
        return vec_ref[i:i + 1, :]

    seg = e_ref[...]
    seg_t = et_ref[...]

    def head_sum(t):
        return _bdot(t, seg)

    def spread(t):
        return _dot_hi_lo(t, seg_t)

    y = y_ref[...].astype(F32)
    yc = y - spread(head_sum(y) * (1.0 / A_HEAD))
    rstd = lax.rsqrt(head_sum(yc * yc) * (1.0 / A_HEAD) + GN_EPS)
    yn = yc * spread(rstd) * vec(_LNW) + vec(_LNB)
    rk = r_ref[...].astype(F32) * k_ref[...].astype(F32) * vec(_RK)
    bonus = spread(head_sum(rk)) * v_ref[...].astype(F32)
    g = g_ref[...].astype(F32)
    o = (yn + bonus) * (g * _sigmoid(g))
    out_ref[...] = x_ref[...] + _bdot(o, wout_ref[...])


def _rwkv_post(y, r, k, v, g, x, vecs, w_out, seg, seg_t):
    t, d = x.shape
    tok = pl.BlockSpec((TM_WIDE, d), lambda i: (i, 0))

    def full(a):
        return pl.BlockSpec(a.shape, lambda i: (0,) * a.ndim)

    return pl.pallas_call(
        _rwkv_post_kernel,
        grid=(t // TM_WIDE,),
        in_specs=[tok] * 6 + [full(vecs), full(w_out), full(seg), full(seg_t)],
        out_specs=tok,
        out_shape=jax.ShapeDtypeStruct((t, d), F32),
        compiler_params=_params(("arbitrary",)),
        name="rwkv_post",
    )(y, r, k, v, g, x, vecs, w_out, seg, seg_t)


def _proj_rope_kernel(scale, rot_t, lin_t, make_tables, x_ref, g_ref, w_ref, *refs):
    xn = _rms(x_ref[...], g_ref[...], NORM_EPS)
    proj = _bdot(xn, w_ref[...])
    d = x_ref.shape[1]
    if make_tables:
        pos_ref, freq_ref, rot_ref, lin_ref, cos_ref, lo_ref, hi_ref = refs
        ang = pos_ref[...] * freq_ref[0:1, :]
        cos = jnp.cos(ang)
        sin = jnp.sin(ang)
        sin_lo = sin * freq_ref[1:2, :]
        sin_hi = sin * freq_ref[2:3, :]
        cos_ref[...] = cos
        lo_ref[...] = sin_lo
        hi_ref[...] = sin_hi
    else:
        cos_ref, lo_ref, hi_ref, rot_ref, lin_ref = refs
        cos, sin_lo, sin_hi = cos_ref[...], lo_ref[...], hi_ref[...]
    if scale != 1.0:
        cos, sin_lo, sin_hi = cos * scale, sin_lo * scale, sin_hi * scale
    half = ROT_DIM // 2
    for c in range(d // LANES):
        cols = slice(c * LANES, (c + 1) * LANES)
        t = proj[:, cols]
        rot = (t * cos + pltpu.roll(t, LANES - half, axis=1) * sin_lo
               + pltpu.roll(t, half, axis=1) * sin_hi)
        if rot_t:
            rot_ref[cols, :] = rot.T.astype(rot_ref.dtype)
        else:
            rot_ref[:, cols] = rot.astype(rot_ref.dtype)
        if lin_t:
            lin_ref[cols, :] = proj[:, d + c * LANES:d + (c + 1) * LANES].T.astype(lin_ref.dtype)
    if not lin_t:
        lin_ref[...] = proj[:, d:].astype(lin_ref.dtype)


def _proj_rope(x, g, w, rope, scale, rot_t, lin_t, lin_dtype):
    t, d = x.shape
    make_tables = len(rope) == 2
    tok = pl.BlockSpec((TM_WIDE, d), lambda i: (i, 0))
    tok_t = pl.BlockSpec((d, TM_WIDE), lambda i: (0, i))
    table = pl.BlockSpec((TM_WIDE, LANES), lambda i: (i, 0))

    def full(a):
        return pl.BlockSpec(a.shape, lambda i: (0,) * a.ndim)

    def shape(transposed, dtype):
        return jax.ShapeDtypeStruct((d, t) if transposed else (t, d), dtype)

    if make_tables:
        rope_specs = [table, full(rope[1])]
    else:
        rope_specs = [table] * 3
    out_specs = [tok_t if rot_t else tok, tok_t if lin_t else tok]
    out_shape = [shape(rot_t, BF16), shape(lin_t, lin_dtype)]
    if make_tables:
        out_specs += [table] * 3
        out_shape += [jax.ShapeDtypeStruct((t, LANES), F32)] * 3
    return pl.pallas_call(
        functools.partial(_proj_rope_kernel, scale, rot_t, lin_t, make_tables),
        grid=(t // TM_WIDE,),
        in_specs=[tok, full(g), full(w)] + rope_specs,
        out_specs=out_specs,
        out_shape=out_shape,
        compiler_params=_params(("arbitrary",)),
        name="proj_rope",
    )(x, g, w, *rope)


def _diff_attn_kernel(lam_init, qt_ref, k_ref, vt_ref, lam_ref, sub_ref, out_ref, s_scr, acc_scr):
    i = pl.program_id(2)
    qt = qt_ref[...]
    row_map = lax.broadcasted_iota(jnp.int32, qt.shape, 0) // B_QK
    zero = jnp.zeros_like(qt)
    qs = jnp.concatenate([jnp.where(row_map == 0, qt, zero), jnp.where(row_map == 1, qt, zero)], axis=1)

    def scores(slot, j):
        kb = k_ref[pl.ds(pl.multiple_of(j * BK, BK), BK), :]
        s = jnp.dot(kb, qs, preferred_element_type=F32)
        s_scr[slot] = s
        return jnp.max(s, axis=0, keepdims=True)

    def update(slot, j, s_max, m_prev, l_prev, diagonal=False):
        s = s_scr[slot]
        if diagonal:
            kpos = lax.broadcasted_iota(jnp.int32, s.shape, 0)
            qpos = lax.broadcasted_iota(jnp.int32, s.shape, 1) % BQ
            s = jnp.where(kpos <= qpos, s, -jnp.inf)
            s_max = jnp.max(s, axis=0, keepdims=True)
        m_new = jnp.maximum(m_prev, s_max)
        p = jnp.exp2(s - m_new)
        alpha = jnp.exp2(m_prev - m_new)
        l_new = alpha * l_prev + jnp.sum(p, axis=0, keepdims=True)
        vb = vt_ref[:, pl.ds(pl.multiple_of(j * BK, BK), BK)]
        acc_scr[...] = alpha * acc_scr[...] + jnp.dot(vb, p.astype(BF16), preferred_element_type=F32)
        return m_new, l_new

    acc_scr[...] = jnp.zeros_like(acc_scr)
    m0 = jnp.full((1, 2 * BQ), -jnp.inf, F32)
    l0 = jnp.zeros((1, 2 * BQ), F32)

    def pair(j, carry):
        max0, m, l = carry
        max1 = scores(1, j + 1)
        m, l = update(0, j, max0, m, l)
        max0 = scores(0, j + 2)
        m, l = update(1, j + 1, max1, m, l)
        return max0, m, l

    carry = (scores(0, 0), m0, l0)
    carry = lax.fori_loop(0, i // 4, lambda t, c: pair(4 * t + 2, pair(4 * t, c)), carry)
    carry = lax.fori_loop(0, (i % 4) // 2, lambda t, c: pair(i - i % 4, c), carry)

    def odd_tail(carry):
        max0, m, l = carry
        scores(1, i)
        m, l = update(0, i - 1, max0, m, l)
        return update(1, i, None, m, l, diagonal=True)

    def even_tail(carry):
        _, m, l = carry
        return update(0, i, None, m, l, diagonal=True)

    m, l = lax.cond(i % 2 == 1, odd_tail, even_tail, carry)

    o_all = acc_scr[...] / l
    lam = (jnp.exp(jnp.sum(lam_ref[0:1, :] * lam_ref[1:2, :], axis=1, keepdims=True))
           - jnp.exp(jnp.sum(lam_ref[2:3, :] * lam_ref[3:4, :], axis=1, keepdims=True)) + lam_init)
    o = (o_all[:, :BQ] - lam * o_all[:, BQ:]).T
    out_ref[...] = (_rms(o, sub_ref[...], SUBLN_EPS) * (1.0 - lam_init)).astype(out_ref.dtype)


def _diff_attn(qt, k, vt, lam_vecs, subln, lam_init, bsz, seq):
    d, t = qt.shape
    nq = seq // BQ
    heads = d // LANES

    def full(a):
        return pl.BlockSpec(a.shape, lambda b, h, i: (0,) * a.ndim)

    return pl.pallas_call(
        functools.partial(_diff_attn_kernel, lam_init),
        grid=(bsz, heads, nq),
        in_specs=[pl.BlockSpec((LANES, BQ), lambda b, h, i: (h, b * nq + i)),
                  pl.BlockSpec((seq, LANES), lambda b, h, i: (b, h)),
                  pl.BlockSpec((LANES, seq), lambda b, h, i: (h, b)),
                  full(lam_vecs), full(subln)],
        out_specs=pl.BlockSpec((BQ, LANES), lambda b, h, i: (b * nq + i, h)),
        out_shape=jax.ShapeDtypeStruct((t, d), BF16),
        scratch_shapes=[pltpu.VMEM((2, BK, 2 * BQ), F32), pltpu.VMEM((LANES, 2 * BQ), F32)],
        compiler_params=_params(("arbitrary", "arbitrary", "arbitrary")),
        name="diff_attn",
    )(qt, k, vt, lam_vecs, subln)


def _attn_post_kernel(final, o_ref, gate_ref, x_ref, wout_ref, fin_ref, out_ref):
    gate = gate_ref[...].astype(F32)
    o = o_ref[...].astype(F32) * (gate * _sigmoid(gate))
    out = x_ref[...] + _bdot(o, wout_ref[...])
    if final:
        out = _rms(out, fin_ref[...], NORM_EPS)
    out_ref[...] = out


def _attn_post(o, gate, x, w_out, fin, final):
    t, d = x.shape
    tok = pl.BlockSpec((TM_WIDE, d), lambda i: (i, 0))

    def full(a):
        return pl.BlockSpec(a.shape, lambda i: (0,) * a.ndim)

    return pl.pallas_call(
        functools.partial(_attn_post_kernel, final),
        grid=(t // TM_WIDE,),
        in_specs=[tok, tok, tok, full(w_out), full(fin)],
        out_specs=tok,
        out_shape=jax.ShapeDtypeStruct((t, d), F32),
        compiler_params=_params(("arbitrary",)),
        name="attn_post",
    )(o, gate, x, w_out, fin)


def _rope_tables():
    lane = jnp.arange(LANES) % B_QK
    half = ROT_DIM // 2
    inv_freq = ROPE_THETA ** (-jnp.arange(0, ROT_DIM, 2, dtype=F32) / ROT_DIM)
    freq = jnp.where(lane < ROT_DIM, inv_freq[lane % half], 0.0)
    lo = jnp.where(lane < half, -1.0, 0.0)
    hi = jnp.where((lane >= half) & (lane < ROT_DIM), 1.0, 0.0)
    return jnp.zeros((8, LANES), F32).at[0].set(freq).at[1].set(lo).at[2].set(hi)


def kernel(x, positions, a_norm, a_mu, a_w_in, a_w0, a_w1, a_w2, a_a0, a_a1, a_a2, a_v0, a_v1, a_v2, a_k_k, a_k_a, a_r_k, a_ln_w, a_ln_b, a_w_out, kv_norm, w_kv, b_norm, b_w_in, b_lq1, b_lk1, b_lq2, b_lk2, b_subln, b_w_out, final_norm):
    bsz, seq, d = x.shape
    t = bsz * seq
    seg = (jnp.arange(d)[:, None] // A_HEAD == jnp.arange(LANES)[None, :]).astype(BF16)
    seg_t = seg.T

    v_first = None
    for layer in range(N_A):
        zero = jnp.zeros((d,), F32)
        v0 = a_v0[layer - 1] if layer > 0 else zero
        vecs = jnp.stack([*a_mu[layer], a_norm[layer], a_w0[layer], a_a0[layer], v0, a_k_k[layer],
                          a_k_a[layer], a_r_k[layer].reshape(d), a_ln_w[layer], a_ln_b[layer], zero])
        vres = None
        if layer > 0:
            vres = (a_v1[layer - 1].astype(BF16), a_v2[layer - 1].astype(BF16), v_first)
        r, lw, k, v, kk, b, g = _rwkv_pre(
            x, vecs, a_w_in[layer].astype(BF16), a_w1[layer].astype(BF16), a_w2[layer].astype(BF16),
            a_a1[layer].astype(BF16), a_a2[layer].astype(BF16), seg, seg_t, vres)
        if layer == 0:
            v_first = v
        y = _wkv(r, lw, k, v, kk, b)
        flat = lambda a: a.reshape(t, d)
        x = _rwkv_post(flat(y), flat(r), flat(k), flat(v), flat(g), flat(x), vecs,
                       a_w_out[layer].astype(BF16), seg, seg_t).reshape(bsz, seq, d)

    x = x.reshape(t, d)
    pos = jnp.broadcast_to(positions.reshape(t, 1).astype(F32), (t, LANES))
    freq = _rope_tables()
    k_sh, vt_sh, *tables = _proj_rope(x, kv_norm.reshape(1, d), w_kv.astype(BF16), (pos, freq), 1.0,
                                      False, True, BF16)
    n_b = b_norm.shape[0]
    for j in range(n_b):
        layer = N_A + j
        lam_init = 0.8 - 0.6 * math.exp(-0.3 * layer)
        qt, gate = _proj_rope(x, b_norm[j].reshape(1, d), b_w_in[j].astype(BF16), tables,
                              B_QK ** -0.5 * math.log2(math.e), True, False, BF16)
        lam_vecs = jnp.zeros((8, LANES), F32).at[:4, :B_QK].set(
            jnp.stack([b_lq1[j], b_lk1[j], b_lq2[j], b_lk2[j]]))
        o = _diff_attn(qt, k_sh, vt_sh, lam_vecs, b_subln[j].reshape(1, B_V), lam_init, bsz, seq)
        x = _attn_post(o, gate, x, b_w_out[j].astype(BF16), final_norm.reshape(1, d), j == n_b - 1)
    return x.reshape(bsz, seq, d)
```

```python
import functools
import math

import jax
import jax.numpy as jnp
from jax import lax
from jax.experimental import pallas as pl
from jax.experimental.pallas import tpu as pltpu

F32 = jnp.float32
BF16 = jnp.bfloat16

A_HEAD = 64
B_QK = 64
B_V = 128
ROT_DIM = B_QK // 4
ROPE_THETA = 500000.0
NORM_EPS = 1e-6
SUBLN_EPS = 1e-5
GN_EPS = 64e-5
N_A = 2

LANES = 128
VMEM_LIMIT = 48 * 1024 * 1024

CHUNK = 64
WKV_CHUNKS = 2
TM = 256
TM_WIDE = 512
BQ = 512
BK = 512
ONES_ROWS = 16

_MU, _NORM, _W0, _A0, _V0, _KK, _KA, _RK, _LNW, _LNB = 0, 6, 7, 8, 9, 10, 11, 12, 13, 14


def _params(sem):
    return pltpu.CompilerParams(dimension_semantics=sem, vmem_limit_bytes=VMEM_LIMIT)


def _bdot(a, b):
    return jnp.dot(a.astype(BF16), b.astype(BF16), preferred_element_type=F32)


def _nt(a, b):
    return lax.dot_general(a.astype(BF16), b.astype(BF16), (((1,), (1,)), ((), ())),
                           preferred_element_type=F32)


def _tn(a, b):
    return lax.dot_general(a.astype(BF16), b.astype(BF16), (((0,), (0,)), ((), ())),
                           preferred_element_type=F32)


def _dot_hi_lo(x, m):
    hi = x.astype(BF16)
    lo = (x - hi.astype(F32)).astype(BF16)
    return (jnp.dot(hi, m, preferred_element_type=F32) + jnp.dot(lo, m, preferred_element_type=F32))


def _rms(x, g, eps):
    return x * lax.rsqrt(jnp.mean(x * x, axis=-1, keepdims=True) + eps) * g


def _sigmoid(x):
    return 1.0 / (1.0 + jnp.exp(-x))


def _rwkv_pre_kernel(has_vres, *refs):
    if has_vres:
        (x_ref, vec_ref, win_ref, w1_ref, w2_ref, a1_ref, a2_ref, e_ref, et_ref,
         v1_ref, v2_ref, vf_ref,
         r_out, lw_out, k_out, v_out, kk_out, b_out, g_out, carry) = refs
    else:
        (x_ref, vec_ref, win_ref, w1_ref, w2_ref, a1_ref, a2_ref, e_ref, et_ref,
         r_out, lw_out, k_out, v_out, kk_out, b_out, g_out, carry) = refs

    def vec(i):
        return vec_ref[i:i + 1, :]

    @pl.when(pl.program_id(1) == 0)
    def _():
        carry[...] = jnp.zeros_like(carry)

    xn = _rms(x_ref[...], vec(_NORM), NORM_EPS)
    tm = xn.shape[0]
    rolled = pltpu.roll(xn, 1, axis=0)
    row = lax.broadcasted_iota(jnp.int32, (tm, 1), 0)
    prev = jnp.where(row == 0, carry[7:8, :], rolled)
    carry[...] = xn[tm - 8:, :]
    xx = prev - xn

    def mix(p):
        return xn + xx * vec(_MU + p)

    r = _bdot(mix(0), win_ref[0])
    k = _bdot(mix(1), win_ref[1])
    xm_v = mix(2)
    v = _bdot(xm_v, win_ref[2])
    g_out[...] = _bdot(mix(3), win_ref[3]).astype(g_out.dtype)

    wl = vec(_W0) + _bdot(jnp.tanh(_bdot(mix(4), w1_ref[...])), w2_ref[...])
    w = -jax.nn.softplus(-wl) - 0.5
    lw_out[...] = -jnp.exp(w)
    lr = _sigmoid(vec(_A0) + _bdot(_bdot(mix(5), a1_ref[...]), a2_ref[...]))
    if has_vres:
        gate_v = _sigmoid(vec(_V0) + _bdot(_bdot(xm_v, v1_ref[...]), v2_ref[...]))
        v = v + (vf_ref[...].astype(F32) - v) * gate_v

    kk = k * vec(_KK)
    ss = _bdot(kk * kk, e_ref[...])
    inv = 1.0 / jnp.maximum(jnp.sqrt(ss), 1e-12)
    kk = kk * _dot_hi_lo(inv, et_ref[...])
    r_out[...] = r.astype(r_out.dtype)
    v_out[...] = v.astype(v_out.dtype)
    kk_out[...] = kk.astype(kk_out.dtype)
    b_out[...] = (kk * lr).astype(b_out.dtype)
    k_out[...] = (k * (1.0 + (lr - 1.0) * vec(_KA))).astype(k_out.dtype)


def _rwkv_pre(x, vecs, w_in, w1, w2, a1, a2, seg, seg_t, vres):
    bsz, seq, d = x.shape
    tok = pl.BlockSpec((None, TM, d), lambda b, j: (b, j, 0))

    def full(a):
        return pl.BlockSpec(a.shape, lambda b, j: (0,) * a.ndim, pipeline_mode=pl.Buffered(1))

    args = [x, vecs, w_in, w1, w2, a1, a2, seg, seg_t]
    specs = [tok] + [full(a) for a in args[1:]]
    if vres is not None:
        v1, v2, v_first = vres
        args += [v1, v2, v_first]
        specs += [full(v1), full(v2), tok]
    dtypes = [BF16, F32, BF16, BF16, BF16, BF16, BF16]
    return pl.pallas_call(
        functools.partial(_rwkv_pre_kernel, vres is not None),
        grid=(bsz, seq // TM),
        in_specs=specs,
        out_specs=[tok] * 7,
        out_shape=[jax.ShapeDtypeStruct((bsz, seq, d), dt) for dt in dtypes],
        scratch_shapes=[pltpu.VMEM((8, d), F32)],
        compiler_params=_params(("arbitrary", "arbitrary")),
        name="rwkv_pre",
    )(*args)


def _stack(x, lane_head):
    x = x.astype(BF16)
    zero = jnp.zeros_like(x)
    return jnp.concatenate([jnp.where(lane_head == 0, x, zero),
                            jnp.where(lane_head == 1, x, zero)], axis=0)


def _wkv_prepare(ops, tri):
    units = range(len(ops))
    lane = lax.broadcasted_iota(jnp.int32, (CHUNK, LANES), 1)
    lane_head = lane // A_HEAD

    def cumsum(lw):
        h1 = lw.astype(BF16)
        r1 = lw - h1.astype(F32)
        h2 = r1.astype(BF16)
        h3 = (r1 - h2.astype(F32)).astype(BF16)
        return (jnp.dot(tri, h1, preferred_element_type=F32) + jnp.dot(tri, h2, preferred_element_type=F32)
                + jnp.dot(tri, h3, preferred_element_type=F32))

    cls = [cumsum(ops[u][1]) for u in units]

    def prep(op, cl):
        r, lw, k, v, kk, b = (t.astype(F32) for t in op)
        e_neg = jnp.exp(-cl)
        e_end = jnp.exp(cl[CHUNK - 1:CHUNK, :])
        bt = b * e_neg
        kt = k * e_neg
        a_n = (-kk * jnp.exp(cl - lw)).astype(BF16)
        r_n = (r * jnp.exp(cl)).astype(BF16)
        return dict(
            e_end=e_end, a_n=a_n, r_n=r_n, v_n=v.astype(BF16), v_s=_stack(v, lane_head),
            ar=jnp.concatenate([a_n, r_n], axis=0),
            bk=jnp.concatenate([_stack(bt, lane_head), _stack(kt, lane_head)], axis=0),
            bkh=jnp.concatenate([bt * e_end, kt * e_end], axis=0).astype(BF16))

    pre = [prep(ops[u], cls[u]) for u in units]
    sc = [_nt(pre[u]["ar"], pre[u]["bk"]) for u in units]

    row = lax.broadcasted_iota(jnp.int32, (CHUNK, LANES), 0)
    col = lane % A_HEAD
    strict = col < row
    incl = col <= row
    eye = (col == row).astype(F32)
    a_ab = [jnp.where(strict, sc[u][:CHUNK, :LANES], 0.0) for u in units]
    a_ak = [jnp.where(strict, sc[u][:CHUNK, LANES:], 0.0).astype(BF16) for u in units]
    a_r = [jnp.concatenate([jnp.where(incl, sc[u][CHUNK:, :LANES], 0.0),
                            jnp.where(incl, sc[u][CHUNK:, LANES:], 0.0)], axis=1).astype(BF16)
           for u in units]

    t_inv = [eye + a_ab[u] for u in units]
    pw = [_bdot(a_ab[u], _stack(a_ab[u], lane_head)) for u in units]
    for _ in range(int(math.log2(CHUNK)) - 2):
        both = [_bdot(jnp.concatenate([t_inv[u], pw[u]], axis=0), _stack(pw[u], lane_head))
                for u in units]
        t_inv = [t_inv[u] + both[u][:CHUNK] for u in units]
        pw = [both[u][CHUNK:] for u in units]
    t_inv = [t_inv[u] + _bdot(t_inv[u], _stack(pw[u], lane_head)) for u in units]

    akv = [_bdot(a_ak[u], pre[u]["v_s"]) for u in units]
    return [dict(pre[u], t_inv=t_inv[u].astype(BF16), akv=akv[u], a_r=a_r[u]) for u in units]


def _wkv_advance(prep, states):
    units = range(len(prep))
    lane_head = lax.broadcasted_iota(jnp.int32, (CHUNK, LANES), 1) // A_HEAD
    st = [states[u].astype(BF16) for u in units]
    ar_st = [_nt(prep[u]["ar"], st[u]) for u in units]
    rhs = [ar_st[u][:CHUNK] + prep[u]["akv"] for u in units]
    us = [jnp.dot(prep[u]["t_inv"], _stack(rhs[u], lane_head), preferred_element_type=F32)
          for u in units]
    uv_n = [jnp.concatenate([us[u].astype(BF16), prep[u]["v_n"]], axis=0) for u in units]
    same_head = (lax.broadcasted_iota(jnp.int32, (LANES, LANES), 0) // A_HEAD
                 == lax.broadcasted_iota(jnp.int32, (LANES, LANES), 1) // A_HEAD)
    new_states = [states[u] * prep[u]["e_end"] + jnp.where(same_head, _tn(uv_n[u], prep[u]["bkh"]), 0.0)
                  for u in units]
    ys = [ar_st[u][CHUNK:]
          + _bdot(prep[u]["a_r"], jnp.concatenate([_stack(us[u], lane_head), prep[u]["v_s"]], axis=0))
          for u in units]
    return ys, new_states


def _wkv_kernel(r_ref, lw_ref, k_ref, v_ref, kk_ref, b_ref, y_ref, state):
    @pl.when(pl.program_id(0) == 0)
    def _():
        state[...] = jnp.zeros_like(state)

    tri = (lax.broadcasted_iota(jnp.int32, (CHUNK, CHUNK), 1)
           <= lax.broadcasted_iota(jnp.int32, (CHUNK, CHUNK), 0)).astype(BF16)
    bsz, _, d = r_ref.shape
    pairs = [(bi, slice(p * LANES, (p + 1) * LANES)) for bi in range(bsz) for p in range(d // LANES)]
    rows = [slice(c * CHUNK, (c + 1) * CHUNK) for c in range(WKV_CHUNKS)]
    prep = _wkv_prepare([tuple(ref[bi, rs, sl] for ref in (r_ref, lw_ref, k_ref, v_ref, kk_ref, b_ref))
                         for rs in rows for bi, sl in pairs], tri)
    states = [state[u] for u in range(len(pairs))]
    for c, rs in enumerate(rows):
        ys, states = _wkv_advance(prep[c * len(pairs):(c + 1) * len(pairs)], states)
        for u, (bi, sl) in enumerate(pairs):
            y_ref[bi, rs, sl] = ys[u].astype(y_ref.dtype)
    for u in range(len(pairs)):
        state[u] = states[u]


def _wkv(r, lw, k, v, kk, b):
    bsz, seq, d = r.shape
    tok = pl.BlockSpec((bsz, WKV_CHUNKS * CHUNK, d), lambda j: (0, j, 0))
    return pl.pallas_call(
        _wkv_kernel,
        grid=(seq // (WKV_CHUNKS * CHUNK),),
        in_specs=[tok] * 6,
        out_specs=tok,
        out_shape=jax.ShapeDtypeStruct((bsz, seq, d), BF16),
        scratch_shapes=[pltpu.VMEM((bsz * d // LANES, LANES, LANES), F32)],
        compiler_params=_params(("arbitrary",)),
        name="wkv",
    )(r, lw, k, v, kk, b)


def _rwkv_post_kernel(y_ref, r_ref, k_ref, v_ref, g_ref, x_ref, vec_ref, wout_ref, e_ref, et_ref,
                      out_ref):
    def vec(i):
        return vec_ref[i:i + 1, :]

    seg = e_ref[...]
    seg_t = et_ref[...]

    def head_sum(t):
        return _bdot(t, seg)

    def spread(t):
        return _dot_hi_lo(t, seg_t)

    y = y_ref[...].astype(F32)
    yc = y - spread(head_sum(y) * (1.0 / A_HEAD))
    rstd = lax.rsqrt(head_sum(yc * yc) * (1.0 / A_HEAD) + GN_EPS)
    yn = yc * spread(rstd) * vec(_LNW) + vec(_LNB)
    rk = r_ref[...].astype(F32) * k_ref[...].astype(F32) * vec(_RK)
    bonus = spread(head_sum(rk)) * v_ref[...].astype(F32)
    g = g_ref[...].astype(F32)
    o = (yn + bonus) * (g * _sigmoid(g))
    out_ref[...] = x_ref[...] + _bdot(o, wout_ref[...])


def _rwkv_post(y, r, k, v, g, x, vecs, w_out, seg, seg_t):
    t, d = x.shape
    tok = pl.BlockSpec((TM_WIDE, d), lambda i: (i, 0))

    def full(a):
        return pl.BlockSpec(a.shape, lambda i: (0,) * a.ndim)

    return pl.pallas_call(
        _rwkv_post_kernel,
        grid=(t // TM_WIDE,),
        in_specs=[tok] * 6 + [full(vecs), full(w_out), full(seg), full(seg_t)],
        out_specs=tok,
        out_shape=jax.ShapeDtypeStruct((t, d), F32),
        compiler_params=_params(("arbitrary",)),
        name="rwkv_post",
    )(y, r, k, v, g, x, vecs, w_out, seg, seg_t)


def _proj_rope_kernel(scale, rot_t, lin_t, make_tables, x_ref, g_ref, w_ref, *refs):
    xn = _rms(x_ref[...], g_ref[...], NORM_EPS)
    proj = _bdot(xn, w_ref[...])
    d = x_ref.shape[1]
    if make_tables:
        pos_ref, freq_ref, rot_ref, lin_ref, cos_ref, lo_ref, hi_ref = refs
        ang = pos_ref[...] * freq_ref[0:1, :]
        cos = jnp.cos(ang)
        sin = jnp.sin(ang)
        sin_lo = sin * freq_ref[1:2, :]
        sin_hi = sin * freq_ref[2:3, :]
        cos_ref[...] = cos
        lo_ref[...] = sin_lo
        hi_ref[...] = sin_hi
    else:
        cos_ref, lo_ref, hi_ref, rot_ref, lin_ref = refs
        cos, sin_lo, sin_hi = cos_ref[...], lo_ref[...], hi_ref[...]
    if scale != 1.0:
        cos, sin_lo, sin_hi = cos * scale, sin_lo * scale, sin_hi * scale
    half = ROT_DIM // 2
    for c in range(d // LANES):
        cols = slice(c * LANES, (c + 1) * LANES)
        t = proj[:, cols]
        rot = (t * cos + pltpu.roll(t, LANES - half, axis=1) * sin_lo
               + pltpu.roll(t, half, axis=1) * sin_hi)
        if rot_t:
            rot_ref[cols, :] = rot.T.astype(rot_ref.dtype)
        else:
            rot_ref[:, cols] = rot.astype(rot_ref.dtype)
        if lin_t:
            lin_ref[cols, :] = proj[:, d + c * LANES:d + (c + 1) * LANES].T.astype(lin_ref.dtype)
    if not lin_t:
        lin_ref[...] = proj[:, d:].astype(lin_ref.dtype)


def _proj_rope(x, g, w, rope, scale, rot_t, lin_t, lin_dtype):
    t, d = x.shape
    make_tables = len(rope) == 2
    tok = pl.BlockSpec((TM_WIDE, d), lambda i: (i, 0))
    tok_t = pl.BlockSpec((d, TM_WIDE), lambda i: (0, i))
    table = pl.BlockSpec((TM_WIDE, LANES), lambda i: (i, 0))

    def full(a):
        return pl.BlockSpec(a.shape, lambda i: (0,) * a.ndim)

    def shape(transposed, dtype):
        return jax.ShapeDtypeStruct((d, t) if transposed else (t, d), dtype)

    if make_tables:
        rope_specs = [table, full(rope[1])]
    else:
        rope_specs = [table] * 3
    out_specs = [tok_t if rot_t else tok, tok_t if lin_t else tok]
    out_shape = [shape(rot_t, BF16), shape(lin_t, lin_dtype)]
    if make_tables:
        out_specs += [table] * 3
        out_shape += [jax.ShapeDtypeStruct((t, LANES), F32)] * 3
    return pl.pallas_call(
        functools.partial(_proj_rope_kernel, scale, rot_t, lin_t, make_tables),
        grid=(t // TM_WIDE,),
        in_specs=[tok, full(g), full(w)] + rope_specs,
        out_specs=out_specs,
        out_shape=out_shape,
        compiler_params=_params(("arbitrary",)),
        name="proj_rope",
    )(x, g, w, *rope)


def _diff_attn_kernel(lam_init, qt_ref, qn_ref, k_ref, vt_ref, lam_ref, sub_ref, bias_ref, out_ref,
                      s_scr, first_max, acc_scr):
    i = pl.program_id(2)

    def stacked(qt):
        row_map = lax.broadcasted_iota(jnp.int32, qt.shape, 0) // B_QK
        zero = jnp.zeros_like(qt)
        return jnp.concatenate([jnp.where(row_map == 0, qt, zero), jnp.where(row_map == 1, qt, zero)],
                               axis=1)

    qs = stacked(qt_ref[...])

    def scores(slot, j, q=qs):
        kb = k_ref[pl.ds(pl.multiple_of(j * BK, BK), BK), :]
        s = jnp.dot(kb, q, preferred_element_type=F32)
        s_scr[slot] = s
        return jnp.max(s, axis=0, keepdims=True)

    def next_first():
        first_max[...] = scores(2, 0, stacked(qn_ref[...]))

    def update(slot, j, s_max, m_prev, l_prev, diagonal=False):
        s = s_scr[slot]
        if diagonal:
            s = s + bias_ref[...]
            s_max = jnp.max(s, axis=0, keepdims=True)
        m_new = jnp.maximum(m_prev, s_max)
        p = jnp.exp2(s - m_new)
        alpha = jnp.exp2(m_prev - m_new)
        vb = jnp.concatenate([vt_ref[:, pl.ds(pl.multiple_of(j * BK, BK), BK)],
                              jnp.ones((ONES_ROWS, BK), BF16)], axis=0)
        pv = jnp.dot(vb, p.astype(BF16), preferred_element_type=F32)
        l_new = alpha * l_prev + pv[LANES:LANES + 1, :]
        acc_scr[...] = alpha * acc_scr[...] + pv[:LANES, :]
        return m_new, l_new

    acc_scr[...] = jnp.zeros_like(acc_scr)
    m0 = jnp.full((1, 2 * BQ), -jnp.inf, F32)
    l0 = jnp.zeros((1, 2 * BQ), F32)

    @pl.when(i == 0)
    def _():
        first_max[...] = scores(2, 0)

    def pair(j, carry):
        max1, m, l = carry
        max0 = scores(0, j + 1)
        m, l = update(1, j, max1, m, l)
        max1 = scores(1, j + 2)
        m, l = update(0, j + 1, max0, m, l)
        return max1, m, l

    def first_tile(_):
        m, l = update(2, 0, None, m0, l0, diagonal=True)
        next_first()
        return m, l

    def general(_):
        max1 = scores(1, 1)
        m, l = update(2, 0, first_max[...], m0, l0)
        n_full = i - 1
        carry = lax.fori_loop(
            0, n_full // 8,
            lambda t, c: pair(8 * t + 7, pair(8 * t + 5, pair(8 * t + 3, pair(8 * t + 1, c)))),
            (max1, m, l))
        base = 1 + n_full - n_full % 8
        carry = lax.fori_loop(0, (n_full % 8) // 2, lambda t, c: pair(base + 2 * t, c), carry)

        def odd_tail(carry):
            max1, m, l = carry
            scores(0, i)
            m, l = update(1, i - 1, max1, m, l)
            next_first()
            return update(0, i, None, m, l, diagonal=True)

        def even_tail(carry):
            _, m, l = carry
            next_first()
            return update(1, i, None, m, l, diagonal=True)

        return lax.cond(n_full % 2 == 1, odd_tail, even_tail, carry)

    m, l = lax.cond(i == 0, first_tile, general, 0)

    o_all = acc_scr[...] / l
    lam = (jnp.exp(jnp.sum(lam_ref[0:1, :] * lam_ref[1:2, :], axis=1, keepdims=True))
           - jnp.exp(jnp.sum(lam_ref[2:3, :] * lam_ref[3:4, :], axis=1, keepdims=True)) + lam_init)
    o = (o_all[:, :BQ] - lam * o_all[:, BQ:]).T
    out_ref[...] = (_rms(o, sub_ref[...], SUBLN_EPS) * (1.0 - lam_init)).astype(out_ref.dtype)


def _diff_attn(qt, k, vt, lam_vecs, subln, lam_init, bsz, seq):
    d, t = qt.shape
    nq = seq // BQ
    heads = d // LANES

    def full(a):
        return pl.BlockSpec(a.shape, lambda b, h, i: (0,) * a.ndim, pipeline_mode=pl.Buffered(1))

    kpos = jnp.arange(BK)[:, None]
    qpos = jnp.arange(2 * BQ)[None, :] % BQ
    bias = jnp.where(kpos <= qpos, 0.0, -jnp.inf).astype(F32)

    return pl.pallas_call(
        functools.partial(_diff_attn_kernel, lam_init),
        grid=(bsz, heads, nq),
        in_specs=[pl.BlockSpec((LANES, BQ), lambda b, h, i: (h, b * nq + i)),
                  pl.BlockSpec((LANES, BQ), lambda b, h, i: (h, b * nq + jnp.minimum(i + 1, nq - 1))),
                  pl.BlockSpec((seq, LANES), lambda b, h, i: (b, h)),
                  pl.BlockSpec((LANES, seq), lambda b, h, i: (h, b)),
                  full(lam_vecs), full(subln), full(bias)],
        out_specs=pl.BlockSpec((BQ, LANES), lambda b, h, i: (b * nq + i, h)),
        out_shape=jax.ShapeDtypeStruct((t, d), BF16),
        scratch_shapes=[pltpu.VMEM((3, BK, 2 * BQ), F32), pltpu.VMEM((1, 2 * BQ), F32),
                        pltpu.VMEM((LANES, 2 * BQ), F32)],
        compiler_params=_params(("arbitrary", "arbitrary", "arbitrary")),
        name="diff_attn",
    )(qt, qt, k, vt, lam_vecs, subln, bias)


def _attn_post_kernel(final, o_ref, gate_ref, x_ref, wout_ref, fin_ref, out_ref):
    gate = gate_ref[...].astype(F32)
    o = o_ref[...].astype(F32) * (gate * _sigmoid(gate))
    out = x_ref[...] + _bdot(o, wout_ref[...])
    if final:
        out = _rms(out, fin_ref[...], NORM_EPS)
    out_ref[...] = out


def _attn_post(o, gate, x, w_out, fin, final):
    t, d = x.shape
    tok = pl.BlockSpec((TM_WIDE, d), lambda i: (i, 0))

    def full(a):
        return pl.BlockSpec(a.shape, lambda i: (0,) * a.ndim)

    return pl.pallas_call(
        functools.partial(_attn_post_kernel, final),
        grid=(t // TM_WIDE,),
        in_specs=[tok, tok, tok, full(w_out), full(fin)],
        out_specs=tok,
        out_shape=jax.ShapeDtypeStruct((t, d), F32),
        compiler_params=_params(("arbitrary",)),
        name="attn_post",
    )(o, gate, x, w_out, fin)


def _rope_tables():
    lane = jnp.arange(LANES) % B_QK
    half = ROT_DIM // 2
    inv_freq = ROPE_THETA ** (-jnp.arange(0, ROT_DIM, 2, dtype=F32) / ROT_DIM)
    freq = jnp.where(lane < ROT_DIM, inv_freq[lane % half], 0.0)
    lo = jnp.where(lane < half, -1.0, 0.0)
    hi = jnp.where((lane >= half) & (lane < ROT_DIM), 1.0, 0.0)
    return jnp.zeros((8, LANES), F32).at[0].set(freq).at[1].set(lo).at[2].set(hi)


def kernel(x, positions, a_norm, a_mu, a_w_in, a_w0, a_w1, a_w2, a_a0, a_a1, a_a2, a_v0, a_v1, a_v2, a_k_k, a_k_a, a_r_k, a_ln_w, a_ln_b, a_w_out, kv_norm, w_kv, b_norm, b_w_in, b_lq1, b_lk1, b_lq2, b_lk2, b_subln, b_w_out, final_norm):
    bsz, seq, d = x.shape
    t = bsz * seq
    seg = (jnp.arange(d)[:, None] // A_HEAD == jnp.arange(LANES)[None, :]).astype(BF16)
    seg_t = seg.T

    v_first = None
    for layer in range(N_A):
        zero = jnp.zeros((d,), F32)
        v0 = a_v0[layer - 1] if layer > 0 else zero
        vecs = jnp.stack([*a_mu[layer], a_norm[layer], a_w0[layer], a_a0[layer], v0, a_k_k[layer],
                          a_k_a[layer], a_r_k[layer].reshape(d), a_ln_w[layer], a_ln_b[layer], zero])
        vres = None
        if layer > 0:
            vres = (a_v1[layer - 1].astype(BF16), a_v2[layer - 1].astype(BF16), v_first)
        r, lw, k, v, kk, b, g = _rwkv_pre(
            x, vecs, a_w_in[layer].astype(BF16), a_w1[layer].astype(BF16), a_w2[layer].astype(BF16),
            a_a1[layer].astype(BF16), a_a2[layer].astype(BF16), seg, seg_t, vres)
        if layer == 0:
            v_first = v
        y = _wkv(r, lw, k, v, kk, b)
        flat = lambda a: a.reshape(t, d)
        x = _rwkv_post(flat(y), flat(r), flat(k), flat(v), flat(g), flat(x), vecs,
                       a_w_out[layer].astype(BF16), seg, seg_t).reshape(bsz, seq, d)

    x = x.reshape(t, d)
    pos = jnp.broadcast_to(positions.reshape(t, 1).astype(F32), (t, LANES))
    freq = _rope_tables()
    k_sh, vt_sh, *tables = _proj_rope(x, kv_norm.reshape(1, d), w_kv.astype(BF16), (pos, freq), 1.0,
                                      False, True, BF16)
    n_b = b_norm.shape[0]
    for j in range(n_b):
        layer = N_A + j
        lam_init = 0.8 - 0.6 * math.exp(-0.3 * layer)
        qt, gate = _proj_rope(x, b_norm[j].reshape(1, d), b_w_in[j].astype(BF16), tables,
                              B_QK ** -0.5 * math.log2(math.e), True, False, BF16)
        lam_vecs = jnp.zeros((8, LANES), F32).at[:4, :B_QK].set(
            jnp.stack([b_lq1[j], b_lk1[j], b_lq2[j], b_lk2[j]]))
        o = _diff_attn(qt, k_sh, vt_sh, lam_vecs, b_subln[j].reshape(1, B_V), lam_init, bsz, seq)
        x = _attn_post(o, gate, x, b_w_out[j].astype(BF16), final_norm.reshape(1, d), j == n_b - 1)
    return x.reshape(bsz, seq, d)
```

```python
import functools
import math

import jax
import jax.numpy as jnp
from jax import lax
from jax.experimental import pallas as pl
from jax.experimental.pallas import tpu as pltpu

F32 = jnp.float32
BF16 = jnp.bfloat16

A_HEAD = 64
B_QK = 64
B_V = 128
ROT_DIM = B_QK // 4
ROPE_THETA = 500000.0
NORM_EPS = 1e-6
SUBLN_EPS = 1e-5
GN_EPS = 64e-5
N_A = 2

LANES = 128
VMEM_LIMIT = 48 * 1024 * 1024

CHUNK = 64
WKV_CHUNKS = 2
TM = 256
TM_WIDE = 512
BQ = 512
BK = 512
ONES_ROWS = 16

_MU, _NORM, _W0, _A0, _V0, _KK, _KA, _RK, _LNW, _LNB = 0, 6, 7, 8, 9, 10, 11, 12, 13, 14


def _params(sem):
    return pltpu.CompilerParams(dimension_semantics=sem, vmem_limit_bytes=VMEM_LIMIT)


def _bdot(a, b):
    return jnp.dot(a.astype(BF16), b.astype(BF16), preferred_element_type=F32)


def _nt(a, b):
    return lax.dot_general(a.astype(BF16), b.astype(BF16), (((1,), (1,)), ((), ())),
                           preferred_element_type=F32)


def _tn(a, b):
    return lax.dot_general(a.astype(BF16), b.astype(BF16), (((0,), (0,)), ((), ())),
                           preferred_element_type=F32)


def _dot_hi_lo(x, m):
    hi = x.astype(BF16)
    lo = (x - hi.astype(F32)).astype(BF16)
    return (jnp.dot(hi, m, preferred_element_type=F32) + jnp.dot(lo, m, preferred_element_type=F32))


def _rms(x, g, eps):
    return x * lax.rsqrt(jnp.mean(x * x, axis=-1, keepdims=True) + eps) * g


def _sigmoid(x):
    return 1.0 / (1.0 + jnp.exp(-x))


def _rwkv_pre_kernel(has_vres, *refs):
    if has_vres:
        (x_ref, vec_ref, win_ref, w1_ref, w2_ref, a1_ref, a2_ref, e_ref, et_ref,
         v1_ref, v2_ref, vf_ref,
         r_out, lw_out, k_out, v_out, kk_out, b_out, g_out, carry) = refs
    else:
        (x_ref, vec_ref, win_ref, w1_ref, w2_ref, a1_ref, a2_ref, e_ref, et_ref,
         r_out, lw_out, k_out, v_out, kk_out, b_out, g_out, carry) = refs

    def vec(i):
        return vec_ref[i:i + 1, :]

    @pl.when(pl.program_id(1) == 0)
    def _():
        carry[...] = jnp.zeros_like(carry)

    xn = _rms(x_ref[...], vec(_NORM), NORM_EPS)
    tm = xn.shape[0]
    rolled = pltpu.roll(xn, 1, axis=0)
    row = lax.broadcasted_iota(jnp.int32, (tm, 1), 0)
    prev = jnp.where(row == 0, carry[7:8, :], rolled)
    carry[...] = xn[tm - 8:, :]
    xx = prev - xn

    def mix(p):
        return xn + xx * vec(_MU + p)

    r = _bdot(mix(0), win_ref[0])
    k = _bdot(mix(1), win_ref[1])
    xm_v = mix(2)
    v = _bdot(xm_v, win_ref[2])
    g_out[...] = _bdot(mix(3), win_ref[3]).astype(g_out.dtype)

    wl = vec(_W0) + _bdot(jnp.tanh(_bdot(mix(4), w1_ref[...])), w2_ref[...])
    w = -jax.nn.softplus(-wl) - 0.5
    lw_out[...] = -jnp.exp(w)
    lr = _sigmoid(vec(_A0) + _bdot(_bdot(mix(5), a1_ref[...]), a2_ref[...]))
    if has_vres:
        gate_v = _sigmoid(vec(_V0) + _bdot(_bdot(xm_v, v1_ref[...]), v2_ref[...]))
        v = v + (vf_ref[...].astype(F32) - v) * gate_v

    kk = k * vec(_KK)
    ss = _bdot(kk * kk, e_ref[...])
    inv = 1.0 / jnp.maximum(jnp.sqrt(ss), 1e-12)
    kk = kk * _dot_hi_lo(inv, et_ref[...])
    r_out[...] = r.astype(r_out.dtype)
    v_out[...] = v.astype(v_out.dtype)
    kk_out[...] = kk.astype(kk_out.dtype)
    b_out[...] = (kk * lr).astype(b_out.dtype)
    k_out[...] = (k * (1.0 + (lr - 1.0) * vec(_KA))).astype(k_out.dtype)


def _rwkv_pre(x, vecs, w_in, w1, w2, a1, a2, seg, seg_t, vres):
    bsz, seq, d = x.shape
    tok = pl.BlockSpec((None, TM, d), lambda b, j: (b, j, 0))

    def full(a):
        return pl.BlockSpec(a.shape, lambda b, j: (0,) * a.ndim, pipeline_mode=pl.Buffered(1))

    args = [x, vecs, w_in, w1, w2, a1, a2, seg, seg_t]
    specs = [tok] + [full(a) for a in args[1:]]
    if vres is not None:
        v1, v2, v_first = vres
        args += [v1, v2, v_first]
        specs += [full(v1), full(v2), tok]
    dtypes = [BF16, F32, BF16, BF16, BF16, BF16, BF16]
    return pl.pallas_call(
        functools.partial(_rwkv_pre_kernel, vres is not None),
        grid=(bsz, seq // TM),
        in_specs=specs,
        out_specs=[tok] * 7,
        out_shape=[jax.ShapeDtypeStruct((bsz, seq, d), dt) for dt in dtypes],
        scratch_shapes=[pltpu.VMEM((8, d), F32)],
        compiler_params=_params(("arbitrary", "arbitrary")),
        name="rwkv_pre",
    )(*args)


def _stack(x, lane_head):
    x = x.astype(BF16)
    zero = jnp.zeros_like(x)
    return jnp.concatenate([jnp.where(lane_head == 0, x, zero),
                            jnp.where(lane_head == 1, x, zero)], axis=0)


def _wkv_prepare(ops, tri):
    units = range(len(ops))
    lane = lax.broadcasted_iota(jnp.int32, (CHUNK, LANES), 1)
    lane_head = lane // A_HEAD

    def cumsum(lw):
        h1 = lw.astype(BF16)
        r1 = lw - h1.astype(F32)
        h2 = r1.astype(BF16)
        h3 = (r1 - h2.astype(F32)).astype(BF16)
        return (jnp.dot(tri, h1, preferred_element_type=F32) + jnp.dot(tri, h2, preferred_element_type=F32)
                + jnp.dot(tri, h3, preferred_element_type=F32))

    cls = [cumsum(ops[u][1]) for u in units]

    def prep(op, cl):
        r, lw, k, v, kk, b = (t.astype(F32) for t in op)
        e_neg = jnp.exp(-cl)
        e_end = jnp.exp(cl[CHUNK - 1:CHUNK, :])
        bt = b * e_neg
        kt = k * e_neg
        a_n = (-kk * jnp.exp(cl - lw)).astype(BF16)
        r_n = (r * jnp.exp(cl)).astype(BF16)
        return dict(
            e_end=e_end, a_n=a_n, r_n=r_n, v_n=v.astype(BF16), v_s=_stack(v, lane_head),
            ar=jnp.concatenate([a_n, r_n], axis=0),
            bk=jnp.concatenate([_stack(bt, lane_head), _stack(kt, lane_head)], axis=0),
            bkh=jnp.concatenate([bt * e_end, kt * e_end], axis=0).astype(BF16))

    pre = [prep(ops[u], cls[u]) for u in units]
    sc = [_nt(pre[u]["ar"], pre[u]["bk"]) for u in units]

    row = lax.broadcasted_iota(jnp.int32, (CHUNK, LANES), 0)
    col = lane % A_HEAD
    strict = col < row
    incl = col <= row
    eye = (col == row).astype(F32)
    a_ab = [jnp.where(strict, sc[u][:CHUNK, :LANES], 0.0) for u in units]
    a_ak = [jnp.where(strict, sc[u][:CHUNK, LANES:], 0.0).astype(BF16) for u in units]
    a_r = [jnp.concatenate([jnp.where(incl, sc[u][CHUNK:, :LANES], 0.0),
                            jnp.where(incl, sc[u][CHUNK:, LANES:], 0.0)], axis=1).astype(BF16)
           for u in units]

    t_inv = [eye + a_ab[u] for u in units]
    pw = [_bdot(a_ab[u], _stack(a_ab[u], lane_head)) for u in units]
    for _ in range(int(math.log2(CHUNK)) - 2):
        both = [_bdot(jnp.concatenate([t_inv[u], pw[u]], axis=0), _stack(pw[u], lane_head))
                for u in units]
        t_inv = [t_inv[u] + both[u][:CHUNK] for u in units]
        pw = [both[u][CHUNK:] for u in units]
    t_inv = [t_inv[u] + _bdot(t_inv[u], _stack(pw[u], lane_head)) for u in units]

    akv = [_bdot(a_ak[u], pre[u]["v_s"]) for u in units]
    return [dict(pre[u], t_inv=t_inv[u].astype(BF16), akv=akv[u], a_r=a_r[u]) for u in units]


def _wkv_advance(prep, states):
    units = range(len(prep))
    lane_head = lax.broadcasted_iota(jnp.int32, (CHUNK, LANES), 1) // A_HEAD
    st = [states[u].astype(BF16) for u in units]
    ar_st = [_nt(prep[u]["ar"], st[u]) for u in units]
    rhs = [ar_st[u][:CHUNK] + prep[u]["akv"] for u in units]
    us = [jnp.dot(prep[u]["t_inv"], _stack(rhs[u], lane_head), preferred_element_type=F32)
          for u in units]
    uv_n = [jnp.concatenate([us[u].astype(BF16), prep[u]["v_n"]], axis=0) for u in units]
    same_head = (lax.broadcasted_iota(jnp.int32, (LANES, LANES), 0) // A_HEAD
                 == lax.broadcasted_iota(jnp.int32, (LANES, LANES), 1) // A_HEAD)
    new_states = [states[u] * prep[u]["e_end"] + jnp.where(same_head, _tn(uv_n[u], prep[u]["bkh"]), 0.0)
                  for u in units]
    ys = [ar_st[u][CHUNK:]
          + _bdot(prep[u]["a_r"], jnp.concatenate([_stack(us[u], lane_head), prep[u]["v_s"]], axis=0))
          for u in units]
    return ys, new_states


def _wkv_kernel(r_ref, lw_ref, k_ref, v_ref, kk_ref, b_ref, y_ref, state):
    @pl.when(pl.program_id(0) == 0)
    def _():
        state[...] = jnp.zeros_like(state)

    tri = (lax.broadcasted_iota(jnp.int32, (CHUNK, CHUNK), 1)
           <= lax.broadcasted_iota(jnp.int32, (CHUNK, CHUNK), 0)).astype(BF16)
    bsz, _, d = r_ref.shape
    pairs = [(bi, slice(p * LANES, (p + 1) * LANES)) for bi in range(bsz) for p in range(d // LANES)]
    rows = [slice(c * CHUNK, (c + 1) * CHUNK) for c in range(WKV_CHUNKS)]
    prep = _wkv_prepare([tuple(ref[bi, rs, sl] for ref in (r_ref, lw_ref, k_ref, v_ref, kk_ref, b_ref))
                         for rs in rows for bi, sl in pairs], tri)
    states = [state[u] for u in range(len(pairs))]
    for c, rs in enumerate(rows):
        ys, states = _wkv_advance(prep[c * len(pairs):(c + 1) * len(pairs)], states)
        for u, (bi, sl) in enumerate(pairs):
            y_ref[bi, rs, sl] = ys[u].astype(y_ref.dtype)
    for u in range(len(pairs)):
        state[u] = states[u]


def _wkv(r, lw, k, v, kk, b):
    bsz, seq, d = r.shape
    tok = pl.BlockSpec((bsz, WKV_CHUNKS * CHUNK, d), lambda j: (0, j, 0))
    return pl.pallas_call(
        _wkv_kernel,
        grid=(seq // (WKV_CHUNKS * CHUNK),),
        in_specs=[tok] * 6,
        out_specs=tok,
        out_shape=jax.ShapeDtypeStruct((bsz, seq, d), BF16),
        scratch_shapes=[pltpu.VMEM((bsz * d // LANES, LANES, LANES), F32)],
        compiler_params=_params(("arbitrary",)),
        name="wkv",
    )(r, lw, k, v, kk, b)


def _rwkv_post_kernel(y_ref, r_ref, k_ref, v_ref, g_ref, x_ref, vec_ref, wout_ref, e_ref, et_ref,
                      out_ref):
    def vec(i):
        return vec_ref[i:i + 1, :]

    seg = e_ref[...]
    seg_t = et_ref[...]

    def head_sum(t):
        return _bdot(t, seg)

    def spread(t):
        return _dot_hi_lo(t, seg_t)

    y = y_ref[...].astype(F32)
    yc = y - spread(head_sum(y) * (1.0 / A_HEAD))
    rstd = lax.rsqrt(head_sum(yc * yc) * (1.0 / A_HEAD) + GN_EPS)
    yn = yc * spread(rstd) * vec(_LNW) + vec(_LNB)
    rk = r_ref[...].astype(F32) * k_ref[...].astype(F32) * vec(_RK)
    bonus = spread(head_sum(rk)) * v_ref[...].astype(F32)
    g = g_ref[...].astype(F32)
    o = (yn + bonus) * (g * _sigmoid(g))
    out_ref[...] = x_ref[...] + _bdot(o, wout_ref[...])


def _rwkv_post(y, r, k, v, g, x, vecs, w_out, seg, seg_t):
    t, d = x.shape
    tok = pl.BlockSpec((TM_WIDE, d), lambda i: (i, 0))

    def full(a):
        return pl.BlockSpec(a.shape, lambda i: (0,) * a.ndim)

    return pl.pallas_call(
        _rwkv_post_kernel,
        grid=(t // TM_WIDE,),
        in_specs=[tok] * 6 + [full(vecs), full(w_out), full(seg), full(seg_t)],
        out_specs=tok,
        out_shape=jax.ShapeDtypeStruct((t, d), F32),
        compiler_params=_params(("arbitrary",)),
        name="rwkv_post",
    )(y, r, k, v, g, x, vecs, w_out, seg, seg_t)


def _proj_rope_kernel(scale, rot_t, lin_t, make_tables, x_ref, g_ref, w_ref, *refs):
    xn = _rms(x_ref[...], g_ref[...], NORM_EPS)
    proj = _bdot(xn, w_ref[...])
    d = x_ref.shape[1]
    if make_tables:
        pos_ref, freq_ref, rot_ref, lin_ref, cos_ref, lo_ref, hi_ref = refs
        ang = pos_ref[...] * freq_ref[0:1, :]
        cos = jnp.cos(ang)
        sin = jnp.sin(ang)
        sin_lo = sin * freq_ref[1:2, :]
        sin_hi = sin * freq_ref[2:3, :]
        cos_ref[...] = cos
        lo_ref[...] = sin_lo
        hi_ref[...] = sin_hi
    else:
        cos_ref, lo_ref, hi_ref, rot_ref, lin_ref = refs
        cos, sin_lo, sin_hi = cos_ref[...], lo_ref[...], hi_ref[...]
    if scale != 1.0:
        cos, sin_lo, sin_hi = cos * scale, sin_lo * scale, sin_hi * scale
    half = ROT_DIM // 2
    for c in range(d // LANES):
        cols = slice(c * LANES, (c + 1) * LANES)
        t = proj[:, cols]
        rot = (t * cos + pltpu.roll(t, LANES - half, axis=1) * sin_lo
               + pltpu.roll(t, half, axis=1) * sin_hi)
        if rot_t:
            rot_ref[cols, :] = rot.T.astype(rot_ref.dtype)
        else:
            rot_ref[:, cols] = rot.astype(rot_ref.dtype)
        if lin_t:
            lin_ref[cols, :] = proj[:, d + c * LANES:d + (c + 1) * LANES].T.astype(lin_ref.dtype)
    if not lin_t:
        lin_ref[...] = proj[:, d:].astype(lin_ref.dtype)


def _proj_rope(x, g, w, rope, scale, rot_t, lin_t, lin_dtype):
    t, d = x.shape
    make_tables = len(rope) == 2
    tok = pl.BlockSpec((TM_WIDE, d), lambda i: (i, 0))
    tok_t = pl.BlockSpec((d, TM_WIDE), lambda i: (0, i))
    table = pl.BlockSpec((TM_WIDE, LANES), lambda i: (i, 0))

    def full(a):
        return pl.BlockSpec(a.shape, lambda i: (0,) * a.ndim)

    def shape(transposed, dtype):
        return jax.ShapeDtypeStruct((d, t) if transposed else (t, d), dtype)

    if make_tables:
        rope_specs = [table, full(rope[1])]
    else:
        rope_specs = [table] * 3
    out_specs = [tok_t if rot_t else tok, tok_t if lin_t else tok]
    out_shape = [shape(rot_t, BF16), shape(lin_t, lin_dtype)]
    if make_tables:
        out_specs += [table] * 3
        out_shape += [jax.ShapeDtypeStruct((t, LANES), F32)] * 3
    return pl.pallas_call(
        functools.partial(_proj_rope_kernel, scale, rot_t, lin_t, make_tables),
        grid=(t // TM_WIDE,),
        in_specs=[tok, full(g), full(w)] + rope_specs,
        out_specs=out_specs,
        out_shape=out_shape,
        compiler_params=_params(("arbitrary",)),
        name="proj_rope",
    )(x, g, w, *rope)


def _diff_attn_kernel(lam_init, qt_ref, qn_ref, k_ref, vt_ref, lam_ref, sub_ref, bias_ref, out_ref,
                      s_scr, first_max, acc_scr):
    i = pl.program_id(2)

    def stacked(qt):
        row_map = lax.broadcasted_iota(jnp.int32, qt.shape, 0) // B_QK
        zero = jnp.zeros_like(qt)
        return jnp.concatenate([jnp.where(row_map == 0, qt, zero), jnp.where(row_map == 1, qt, zero)],
                               axis=1)

    qs = stacked(qt_ref[...])
    lam = (jnp.exp(jnp.sum(lam_ref[0:1, :] * lam_ref[1:2, :], axis=1, keepdims=True))
           - jnp.exp(jnp.sum(lam_ref[2:3, :] * lam_ref[3:4, :], axis=1, keepdims=True)) + lam_init)

    def scores(slot, j, q=qs):
        kb = k_ref[pl.ds(pl.multiple_of(j * BK, BK), BK), :]
        s = jnp.dot(kb, q, preferred_element_type=F32)
        s_scr[slot] = s
        return jnp.max(s, axis=0, keepdims=True)

    def next_first():
        first_max[...] = scores(2, 0, stacked(qn_ref[...]))

    def update(slot, j, s_max, m_prev, l_prev, diagonal=False):
        s = s_scr[slot]
        if diagonal:
            s = s + bias_ref[...]
            s_max = jnp.max(s, axis=0, keepdims=True)
        m_new = jnp.maximum(m_prev, s_max)
        p = jnp.exp2(s - m_new)
        alpha = jnp.exp2(m_prev - m_new)
        vb = jnp.concatenate([vt_ref[:, pl.ds(pl.multiple_of(j * BK, BK), BK)],
                              jnp.ones((ONES_ROWS, BK), BF16)], axis=0)
        pv = jnp.dot(vb, p.astype(BF16), preferred_element_type=F32)
        l_new = alpha * l_prev + pv[LANES:LANES + 1, :]
        acc_scr[...] = alpha * acc_scr[...] + pv[:LANES, :]
        return m_new, l_new

    acc_scr[...] = jnp.zeros_like(acc_scr)
    m0 = jnp.full((1, 2 * BQ), -jnp.inf, F32)
    l0 = jnp.zeros((1, 2 * BQ), F32)

    @pl.when(i == 0)
    def _():
        first_max[...] = scores(2, 0)

    def pair(j, carry):
        max1, m, l = carry
        max0 = scores(0, j + 1)
        m, l = update(1, j, max1, m, l)
        max1 = scores(1, j + 2)
        m, l = update(0, j + 1, max0, m, l)
        return max1, m, l

    def first_tile(_):
        m, l = update(2, 0, None, m0, l0, diagonal=True)
        next_first()
        return m, l

    def general(_):
        max1 = scores(1, 1)
        m, l = update(2, 0, first_max[...], m0, l0)
        n_full = i - 1
        carry = lax.fori_loop(
            0, n_full // 8,
            lambda t, c: pair(8 * t + 7, pair(8 * t + 5, pair(8 * t + 3, pair(8 * t + 1, c)))),
            (max1, m, l))
        base = 1 + n_full - n_full % 8
        carry = lax.fori_loop(0, (n_full % 8) // 2, lambda t, c: pair(base + 2 * t, c), carry)

        def odd_tail(carry):
            max1, m, l = carry
            scores(0, i)
            m, l = update(1, i - 1, max1, m, l)
            next_first()
            return update(0, i, None, m, l, diagonal=True)

        def even_tail(carry):
            _, m, l = carry
            next_first()
            return update(1, i, None, m, l, diagonal=True)

        return lax.cond(n_full % 2 == 1, odd_tail, even_tail, carry)

    m, l = lax.cond(i == 0, first_tile, general, 0)

    o_all = acc_scr[...] / l
    o_t = o_all[:, :BQ] - lam * o_all[:, BQ:]
    o_t = o_t * lax.rsqrt(jnp.mean(o_t * o_t, axis=0, keepdims=True) + SUBLN_EPS)
    out_ref[...] = (o_t.T * (sub_ref[...] * (1.0 - lam_init))).astype(out_ref.dtype)


def _diff_attn(qt, k, vt, lam_vecs, subln, lam_init, bsz, seq):
    d, t = qt.shape
    nq = seq // BQ
    heads = d // LANES

    def full(a):
        return pl.BlockSpec(a.shape, lambda b, h, i: (0,) * a.ndim, pipeline_mode=pl.Buffered(1))

    kpos = jnp.arange(BK)[:, None]
    qpos = jnp.arange(2 * BQ)[None, :] % BQ
    bias = jnp.where(kpos <= qpos, 0.0, -jnp.inf).astype(F32)

    return pl.pallas_call(
        functools.partial(_diff_attn_kernel, lam_init),
        grid=(bsz, heads, nq),
        in_specs=[pl.BlockSpec((LANES, BQ), lambda b, h, i: (h, b * nq + i)),
                  pl.BlockSpec((LANES, BQ), lambda b, h, i: (h, b * nq + jnp.minimum(i + 1, nq - 1))),
                  pl.BlockSpec((seq, LANES), lambda b, h, i: (b, h)),
                  pl.BlockSpec((LANES, seq), lambda b, h, i: (h, b)),
                  full(lam_vecs), full(subln), full(bias)],
        out_specs=pl.BlockSpec((BQ, LANES), lambda b, h, i: (b * nq + i, h)),
        out_shape=jax.ShapeDtypeStruct((t, d), BF16),
        scratch_shapes=[pltpu.VMEM((3, BK, 2 * BQ), F32), pltpu.VMEM((1, 2 * BQ), F32),
                        pltpu.VMEM((LANES, 2 * BQ), F32)],
        compiler_params=_params(("arbitrary", "arbitrary", "arbitrary")),
        name="diff_attn",
    )(qt, qt, k, vt, lam_vecs, subln, bias)


def _attn_post_kernel(final, o_ref, gate_ref, x_ref, wout_ref, fin_ref, out_ref):
    gate = gate_ref[...].astype(F32)
    o = o_ref[...].astype(F32) * (gate * _sigmoid(gate))
    out = x_ref[...] + _bdot(o, wout_ref[...])
    if final:
        out = _rms(out, fin_ref[...], NORM_EPS)
    out_ref[...] = out


def _attn_post(o, gate, x, w_out, fin, final):
    t, d = x.shape
    tok = pl.BlockSpec((TM_WIDE, d), lambda i: (i, 0))

    def full(a):
        return pl.BlockSpec(a.shape, lambda i: (0,) * a.ndim)

    return pl.pallas_call(
        functools.partial(_attn_post_kernel, final),
        grid=(t // TM_WIDE,),
        in_specs=[tok, tok, tok, full(w_out), full(fin)],
        out_specs=tok,
        out_shape=jax.ShapeDtypeStruct((t, d), F32),
        compiler_params=_params(("arbitrary",)),
        name="attn_post",
    )(o, gate, x, w_out, fin)


def _rope_tables():
    lane = jnp.arange(LANES) % B_QK
    half = ROT_DIM // 2
    inv_freq = ROPE_THETA ** (-jnp.arange(0, ROT_DIM, 2, dtype=F32) / ROT_DIM)
    freq = jnp.where(lane < ROT_DIM, inv_freq[lane % half], 0.0)
    lo = jnp.where(lane < half, -1.0, 0.0)
    hi = jnp.where((lane >= half) & (lane < ROT_DIM), 1.0, 0.0)
    return jnp.zeros((8, LANES), F32).at[0].set(freq).at[1].set(lo).at[2].set(hi)


def kernel(x, positions, a_norm, a_mu, a_w_in, a_w0, a_w1, a_w2, a_a0, a_a1, a_a2, a_v0, a_v1, a_v2, a_k_k, a_k_a, a_r_k, a_ln_w, a_ln_b, a_w_out, kv_norm, w_kv, b_norm, b_w_in, b_lq1, b_lk1, b_lq2, b_lk2, b_subln, b_w_out, final_norm):
    bsz, seq, d = x.shape
    t = bsz * seq
    seg = (jnp.arange(d)[:, None] // A_HEAD == jnp.arange(LANES)[None, :]).astype(BF16)
    seg_t = seg.T

    v_first = None
    for layer in range(N_A):
        zero = jnp.zeros((d,), F32)
        v0 = a_v0[layer - 1] if layer > 0 else zero
        vecs = jnp.stack([*a_mu[layer], a_norm[layer], a_w0[layer], a_a0[layer], v0, a_k_k[layer],
                          a_k_a[layer], a_r_k[layer].reshape(d), a_ln_w[layer], a_ln_b[layer], zero])
        vres = None
        if layer > 0:
            vres = (a_v1[layer - 1].astype(BF16), a_v2[layer - 1].astype(BF16), v_first)
        r, lw, k, v, kk, b, g = _rwkv_pre(
            x, vecs, a_w_in[layer].astype(BF16), a_w1[layer].astype(BF16), a_w2[layer].astype(BF16),
            a_a1[layer].astype(BF16), a_a2[layer].astype(BF16), seg, seg_t, vres)
        if layer == 0:
            v_first = v
        y = _wkv(r, lw, k, v, kk, b)
        flat = lambda a: a.reshape(t, d)
        x = _rwkv_post(flat(y), flat(r), flat(k), flat(v), flat(g), flat(x), vecs,
                       a_w_out[layer].astype(BF16), seg, seg_t).reshape(bsz, seq, d)

    x = x.reshape(t, d)
    pos = jnp.broadcast_to(positions.reshape(t, 1).astype(F32), (t, LANES))
    freq = _rope_tables()
    k_sh, vt_sh, *tables = _proj_rope(x, kv_norm.reshape(1, d), w_kv.astype(BF16), (pos, freq), 1.0,
                                      False, True, BF16)
    n_b = b_norm.shape[0]
    for j in range(n_b):
        layer = N_A + j
        lam_init = 0.8 - 0.6 * math.exp(-0.3 * layer)
        qt, gate = _proj_rope(x, b_norm[j].reshape(1, d), b_w_in[j].astype(BF16), tables,
                              B_QK ** -0.5 * math.log2(math.e), True, False, BF16)
        lam_vecs = jnp.zeros((8, LANES), F32).at[:4, :B_QK].set(
            jnp.stack([b_lq1[j], b_lk1[j], b_lq2[j], b_lk2[j]]))
        o = _diff_attn(qt, k_sh, vt_sh, lam_vecs, b_subln[j].reshape(1, B_V), lam_init, bsz, seq)
        x = _attn_post(o, gate, x, b_w_out[j].astype(BF16), final_norm.reshape(1, d), j == n_b - 1)
    return x.reshape(bsz, seq, d)
```

```python
import functools
import math

import jax
import jax.numpy as jnp
from jax import lax
from jax.experimental import pallas as pl
from jax.experimental.pallas import tpu as pltpu

F32 = jnp.float32
BF16 = jnp.bfloat16

A_HEAD = 64
B_QK = 64
B_V = 128
ROT_DIM = B_QK // 4
ROPE_THETA = 500000.0
NORM_EPS = 1e-6
SUBLN_EPS = 1e-5
GN_EPS = 64e-5
N_A = 2

LANES = 128
VMEM_LIMIT = 48 * 1024 * 1024

CHUNK = 64
WKV_CHUNKS = 4
TM = 256
TM_WIDE = 512
BQ = 512
BK = 512
ONES_ROWS = 16

_MU, _NORM, _W0, _A0, _V0, _KK, _KA, _RK, _LNW, _LNB = 0, 6, 7, 8, 9, 10, 11, 12, 13, 14


def _params(sem):
    return pltpu.CompilerParams(dimension_semantics=sem, vmem_limit_bytes=VMEM_LIMIT)


def _bdot(a, b):
    return jnp.dot(a.astype(BF16), b.astype(BF16), preferred_element_type=F32)


def _nt(a, b):
    return lax.dot_general(a.astype(BF16), b.astype(BF16), (((1,), (1,)), ((), ())),
                           preferred_element_type=F32)


def _tn(a, b):
    return lax.dot_general(a.astype(BF16), b.astype(BF16), (((0,), (0,)), ((), ())),
                           preferred_element_type=F32)


def _dot_hi_lo(x, m):
    hi = x.astype(BF16)
    lo = (x - hi.astype(F32)).astype(BF16)
    return (jnp.dot(hi, m, preferred_element_type=F32) + jnp.dot(lo, m, preferred_element_type=F32))


def _rms(x, g, eps):
    return x * lax.rsqrt(jnp.mean(x * x, axis=-1, keepdims=True) + eps) * g


def _sigmoid(x):
    return 1.0 / (1.0 + jnp.exp(-x))


def _rwkv_pre_kernel(has_vres, *refs):
    if has_vres:
        (x_ref, vec_ref, win_ref, w1_ref, w2_ref, a1_ref, a2_ref, e_ref, et_ref,
         v1_ref, v2_ref, vf_ref,
         r_out, lw_out, k_out, v_out, kk_out, b_out, g_out, carry) = refs
    else:
        (x_ref, vec_ref, win_ref, w1_ref, w2_ref, a1_ref, a2_ref, e_ref, et_ref,
         r_out, lw_out, k_out, v_out, kk_out, b_out, g_out, carry) = refs

    def vec(i):
        return vec_ref[i:i + 1, :]

    @pl.when(pl.program_id(1) == 0)
    def _():
        carry[...] = jnp.zeros_like(carry)

    xn = _rms(x_ref[...], vec(_NORM), NORM_EPS)
    tm = xn.shape[0]
    rolled = pltpu.roll(xn, 1, axis=0)
    row = lax.broadcasted_iota(jnp.int32, (tm, 1), 0)
    prev = jnp.where(row == 0, carry[7:8, :], rolled)
    carry[...] = xn[tm - 8:, :]
    xx = prev - xn

    def mix(p):
        return xn + xx * vec(_MU + p)

    r = _bdot(mix(0), win_ref[0])
    k = _bdot(mix(1), win_ref[1])
    xm_v = mix(2)
    v = _bdot(xm_v, win_ref[2])
    g_out[...] = _bdot(mix(3), win_ref[3]).astype(g_out.dtype)

    wl = vec(_W0) + _bdot(jnp.tanh(_bdot(mix(4), w1_ref[...])), w2_ref[...])
    lw_out[...] = -math.exp(-0.5) * _sigmoid(wl)
    lr = _sigmoid(vec(_A0) + _bdot(_bdot(mix(5), a1_ref[...]), a2_ref[...]))
    if has_vres:
        gate_v = _sigmoid(vec(_V0) + _bdot(_bdot(xm_v, v1_ref[...]), v2_ref[...]))
        v = v + (vf_ref[...].astype(F32) - v) * gate_v

    kk = k * vec(_KK)
    ss = _bdot(kk * kk, e_ref[...])
    inv = 1.0 / jnp.maximum(jnp.sqrt(ss), 1e-12)
    kk = kk * _dot_hi_lo(inv, et_ref[...])
    r_out[...] = r.astype(r_out.dtype)
    v_out[...] = v.astype(v_out.dtype)
    kk_out[...] = kk.astype(kk_out.dtype)
    b_out[...] = (kk * lr).astype(b_out.dtype)
    k_out[...] = (k * (1.0 + (lr - 1.0) * vec(_KA))).astype(k_out.dtype)


def _rwkv_pre(x, vecs, w_in, w1, w2, a1, a2, seg, seg_t, vres):
    bsz, seq, d = x.shape
    tok = pl.BlockSpec((None, TM, d), lambda b, j: (b, j, 0))

    def full(a):
        return pl.BlockSpec(a.shape, lambda b, j: (0,) * a.ndim, pipeline_mode=pl.Buffered(1))

    args = [x, vecs, w_in, w1, w2, a1, a2, seg, seg_t]
    specs = [tok] + [full(a) for a in args[1:]]
    if vres is not None:
        v1, v2, v_first = vres
        args += [v1, v2, v_first]
        specs += [full(v1), full(v2), tok]
    dtypes = [BF16, F32, BF16, BF16, BF16, BF16, BF16]
    return pl.pallas_call(
        functools.partial(_rwkv_pre_kernel, vres is not None),
        grid=(bsz, seq // TM),
        in_specs=specs,
        out_specs=[tok] * 7,
        out_shape=[jax.ShapeDtypeStruct((bsz, seq, d), dt) for dt in dtypes],
        scratch_shapes=[pltpu.VMEM((8, d), F32)],
        compiler_params=_params(("arbitrary", "arbitrary")),
        name="rwkv_pre",
    )(*args)


def _stack(x, lane_head):
    x = x.astype(BF16)
    zero = jnp.zeros_like(x)
    return jnp.concatenate([jnp.where(lane_head == 0, x, zero),
                            jnp.where(lane_head == 1, x, zero)], axis=0)


def _wkv_prepare(ops, tri):
    units = range(len(ops))
    lane = lax.broadcasted_iota(jnp.int32, (CHUNK, LANES), 1)
    lane_head = lane // A_HEAD

    def cumsum(lw):
        h1 = lw.astype(BF16)
        r1 = lw - h1.astype(F32)
        h2 = r1.astype(BF16)
        h3 = (r1 - h2.astype(F32)).astype(BF16)
        return (jnp.dot(tri, h1, preferred_element_type=F32) + jnp.dot(tri, h2, preferred_element_type=F32)
                + jnp.dot(tri, h3, preferred_element_type=F32))

    cls = [cumsum(ops[u][1]) for u in units]

    def prep(op, cl):
        r, lw, k, v, kk, b = (t.astype(F32) for t in op)
        e_neg = jnp.exp(-cl)
        e_end = jnp.exp(cl[CHUNK - 1:CHUNK, :])
        bt = b * e_neg
        kt = k * e_neg
        a_n = (-kk * jnp.exp(cl - lw)).astype(BF16)
        r_n = (r * jnp.exp(cl)).astype(BF16)
        return dict(
            e_end=e_end, a_n=a_n, r_n=r_n, v_n=v.astype(BF16), v_s=_stack(v, lane_head),
            ar=jnp.concatenate([a_n, r_n], axis=0),
            bk=jnp.concatenate([_stack(bt, lane_head), _stack(kt, lane_head)], axis=0),
            bkh=jnp.concatenate([bt * e_end, kt * e_end], axis=0).astype(BF16))

    pre = [prep(ops[u], cls[u]) for u in units]
    sc = [_nt(pre[u]["ar"], pre[u]["bk"]) for u in units]

    row = lax.broadcasted_iota(jnp.int32, (CHUNK, LANES), 0)
    col = lane % A_HEAD
    strict = col < row
    incl = col <= row
    eye = (col == row).astype(F32)
    a_ab = [jnp.where(strict, sc[u][:CHUNK, :LANES], 0.0) for u in units]
    a_ak = [jnp.where(strict, sc[u][:CHUNK, LANES:], 0.0).astype(BF16) for u in units]
    a_r = [jnp.concatenate([jnp.where(incl, sc[u][CHUNK:, :LANES], 0.0),
                            jnp.where(incl, sc[u][CHUNK:, LANES:], 0.0)], axis=1).astype(BF16)
           for u in units]

    t_inv = [eye + a_ab[u] for u in units]
    pw = [_bdot(a_ab[u], _stack(a_ab[u], lane_head)) for u in units]
    for _ in range(int(math.log2(CHUNK)) - 2):
        both = [_bdot(jnp.concatenate([t_inv[u], pw[u]], axis=0), _stack(pw[u], lane_head))
                for u in units]
        t_inv = [t_inv[u] + both[u][:CHUNK] for u in units]
        pw = [both[u][CHUNK:] for u in units]
    t_inv = [t_inv[u] + _bdot(t_inv[u], _stack(pw[u], lane_head)) for u in units]

    akv = [_bdot(a_ak[u], pre[u]["v_s"]) for u in units]
    return [dict(pre[u], t_inv=t_inv[u].astype(BF16), akv=akv[u], a_r=a_r[u]) for u in units]


def _wkv_advance(prep, states):
    units = range(len(prep))
    lane_head = lax.broadcasted_iota(jnp.int32, (CHUNK, LANES), 1) // A_HEAD
    st = [states[u].astype(BF16) for u in units]
    ar_st = [_nt(prep[u]["ar"], st[u]) for u in units]
    rhs = [ar_st[u][:CHUNK] + prep[u]["akv"] for u in units]
    us = [jnp.dot(prep[u]["t_inv"], _stack(rhs[u], lane_head), preferred_element_type=F32)
          for u in units]
    uv_n = [jnp.concatenate([us[u].astype(BF16), prep[u]["v_n"]], axis=0) for u in units]
    same_head = (lax.broadcasted_iota(jnp.int32, (LANES, LANES), 0) // A_HEAD
                 == lax.broadcasted_iota(jnp.int32, (LANES, LANES), 1) // A_HEAD)
    new_states = [states[u] * prep[u]["e_end"] + jnp.where(same_head, _tn(uv_n[u], prep[u]["bkh"]), 0.0)
                  for u in units]
    ys = [ar_st[u][CHUNK:]
          + _bdot(prep[u]["a_r"], jnp.concatenate([_stack(us[u], lane_head), prep[u]["v_s"]], axis=0))
          for u in units]
    return ys, new_states


def _wkv_kernel(r_ref, lw_ref, k_ref, v_ref, kk_ref, b_ref, y_ref, state):
    @pl.when(pl.program_id(0) == 0)
    def _():
        state[...] = jnp.zeros_like(state)

    tri = (lax.broadcasted_iota(jnp.int32, (CHUNK, CHUNK), 1)
           <= lax.broadcasted_iota(jnp.int32, (CHUNK, CHUNK), 0)).astype(BF16)
    bsz, _, d = r_ref.shape
    pairs = [(bi, slice(p * LANES, (p + 1) * LANES)) for bi in range(bsz) for p in range(d // LANES)]
    rows = [slice(c * CHUNK, (c + 1) * CHUNK) for c in range(WKV_CHUNKS)]
    prep = _wkv_prepare([tuple(ref[bi, rs, sl] for ref in (r_ref, lw_ref, k_ref, v_ref, kk_ref, b_ref))
                         for rs in rows for bi, sl in pairs], tri)
    states = [state[u] for u in range(len(pairs))]
    for c, rs in enumerate(rows):
        ys, states = _wkv_advance(prep[c * len(pairs):(c + 1) * len(pairs)], states)
        for u, (bi, sl) in enumerate(pairs):
            y_ref[bi, rs, sl] = ys[u].astype(y_ref.dtype)
    for u in range(len(pairs)):
        state[u] = states[u]


def _wkv(r, lw, k, v, kk, b):
    bsz, seq, d = r.shape
    tok = pl.BlockSpec((bsz, WKV_CHUNKS * CHUNK, d), lambda j: (0, j, 0))
    return pl.pallas_call(
        _wkv_kernel,
        grid=(seq // (WKV_CHUNKS * CHUNK),),
        in_specs=[tok] * 6,
        out_specs=tok,
        out_shape=jax.ShapeDtypeStruct((bsz, seq, d), BF16),
        scratch_shapes=[pltpu.VMEM((bsz * d // LANES, LANES, LANES), F32)],
        compiler_params=_params(("arbitrary",)),
        name="wkv",
    )(r, lw, k, v, kk, b)


def _rwkv_post_kernel(y_ref, r_ref, k_ref, v_ref, g_ref, x_ref, vec_ref, wout_ref, e_ref, et_ref,
                      out_ref):
    def vec(i):
        return vec_ref[i:i + 1, :]

    seg = e_ref[...]
    seg_t = et_ref[...]

    def head_sum(t):
        return _bdot(t, seg)

    def spread(t):
        return _dot_hi_lo(t, seg_t)

    y = y_ref[...].astype(F32)
    yc = y - spread(head_sum(y) * (1.0 / A_HEAD))
    rstd = lax.rsqrt(head_sum(yc * yc) * (1.0 / A_HEAD) + GN_EPS)
    yn = yc * spread(rstd) * vec(_LNW) + vec(_LNB)
    rk = r_ref[...].astype(F32) * k_ref[...].astype(F32) * vec(_RK)
    bonus = spread(head_sum(rk)) * v_ref[...].astype(F32)
    g = g_ref[...].astype(F32)
    o = (yn + bonus) * (g * _sigmoid(g))
    out_ref[...] = x_ref[...] + _bdot(o, wout_ref[...])


def _rwkv_post(y, r, k, v, g, x, vecs, w_out, seg, seg_t):
    t, d = x.shape
    tok = pl.BlockSpec((TM_WIDE, d), lambda i: (i, 0))

    def full(a):
        return pl.BlockSpec(a.shape, lambda i: (0,) * a.ndim)

    return pl.pallas_call(
        _rwkv_post_kernel,
        grid=(t // TM_WIDE,),
        in_specs=[tok] * 6 + [full(vecs), full(w_out), full(seg), full(seg_t)],
        out_specs=tok,
        out_shape=jax.ShapeDtypeStruct((t, d), F32),
        compiler_params=_params(("arbitrary",)),
        name="rwkv_post",
    )(y, r, k, v, g, x, vecs, w_out, seg, seg_t)


def _proj_rope_kernel(scale, rot_t, lin_t, make_tables, x_ref, g_ref, w_ref, *refs):
    xn = _rms(x_ref[...], g_ref[...], NORM_EPS)
    proj = _bdot(xn, w_ref[...])
    d = x_ref.shape[1]
    if make_tables:
        pos_ref, freq_ref, rot_ref, lin_ref, cos_ref, lo_ref, hi_ref = refs
        ang = pos_ref[...] * freq_ref[0:1, :]
        cos = jnp.cos(ang)
        sin = jnp.sin(ang)
        sin_lo = sin * freq_ref[1:2, :]
        sin_hi = sin * freq_ref[2:3, :]
        cos_ref[...] = cos
        lo_ref[...] = sin_lo
        hi_ref[...] = sin_hi
    else:
        cos_ref, lo_ref, hi_ref, rot_ref, lin_ref = refs
        cos, sin_lo, sin_hi = cos_ref[...], lo_ref[...], hi_ref[...]
    if scale != 1.0:
        cos, sin_lo, sin_hi = cos * scale, sin_lo * scale, sin_hi * scale
    half = ROT_DIM // 2
    for c in range(d // LANES):
        cols = slice(c * LANES, (c + 1) * LANES)
        t = proj[:, cols]
        rot = (t * cos + pltpu.roll(t, LANES - half, axis=1) * sin_lo
               + pltpu.roll(t, half, axis=1) * sin_hi)
        if rot_t:
            rot_ref[cols, :] = rot.T.astype(rot_ref.dtype)
        else:
            rot_ref[:, cols] = rot.astype(rot_ref.dtype)
        if lin_t:
            lin_ref[cols, :] = proj[:, d + c * LANES:d + (c + 1) * LANES].T.astype(lin_ref.dtype)
    if not lin_t:
        lin_ref[...] = proj[:, d:].astype(lin_ref.dtype)


def _proj_rope(x, g, w, rope, scale, rot_t, lin_t, lin_dtype):
    t, d = x.shape
    make_tables = len(rope) == 2
    tok = pl.BlockSpec((TM_WIDE, d), lambda i: (i, 0))
    tok_t = pl.BlockSpec((d, TM_WIDE), lambda i: (0, i))
    table = pl.BlockSpec((TM_WIDE, LANES), lambda i: (i, 0))

    def full(a):
        return pl.BlockSpec(a.shape, lambda i: (0,) * a.ndim)

    def shape(transposed, dtype):
        return jax.ShapeDtypeStruct((d, t) if transposed else (t, d), dtype)

    if make_tables:
        rope_specs = [table, full(rope[1])]
    else:
        rope_specs = [table] * 3
    out_specs = [tok_t if rot_t else tok, tok_t if lin_t else tok]
    out_shape = [shape(rot_t, BF16), shape(lin_t, lin_dtype)]
    if make_tables:
        out_specs += [table] * 3
        out_shape += [jax.ShapeDtypeStruct((t, LANES), F32)] * 3
    return pl.pallas_call(
        functools.partial(_proj_rope_kernel, scale, rot_t, lin_t, make_tables),
        grid=(t // TM_WIDE,),
        in_specs=[tok, full(g), full(w)] + rope_specs,
        out_specs=out_specs,
        out_shape=out_shape,
        compiler_params=_params(("arbitrary",)),
        name="proj_rope",
    )(x, g, w, *rope)


def _diff_attn_kernel(lam_init, qt_ref, qn_ref, k_ref, vt_ref, lam_ref, sub_ref, bias_ref, out_ref,
                      s_scr, first_max, acc_scr):
    i = pl.program_id(2)

    def stacked(qt):
        row_map = lax.broadcasted_iota(jnp.int32, qt.shape, 0) // B_QK
        zero = jnp.zeros_like(qt)
        return jnp.concatenate([jnp.where(row_map == 0, qt, zero), jnp.where(row_map == 1, qt, zero)],
                               axis=1)

    qs = stacked(qt_ref[...])
    lam = (jnp.exp(jnp.sum(lam_ref[0:1, :] * lam_ref[1:2, :], axis=1, keepdims=True))
           - jnp.exp(jnp.sum(lam_ref[2:3, :] * lam_ref[3:4, :], axis=1, keepdims=True)) + lam_init)

    def scores(slot, j, q=qs):
        kb = k_ref[pl.ds(pl.multiple_of(j * BK, BK), BK), :]
        s = jnp.dot(kb, q, preferred_element_type=F32)
        s_scr[slot] = s
        return jnp.max(s, axis=0, keepdims=True)

    def next_first():
        first_max[...] = scores(2, 0, stacked(qn_ref[...]))

    def update(slot, j, s_max, m_prev, l_prev, diagonal=False):
        s = s_scr[slot]
        if diagonal:
            s = s + bias_ref[...]
            s_max = jnp.max(s, axis=0, keepdims=True)
        m_new = jnp.maximum(m_prev, s_max)
        p = jnp.exp2(s - m_new)
        alpha = jnp.exp2(m_prev - m_new)
        vb = jnp.concatenate([vt_ref[:, pl.ds(pl.multiple_of(j * BK, BK), BK)],
                              jnp.ones((ONES_ROWS, BK), BF16)], axis=0)
        pv = jnp.dot(vb, p.astype(BF16), preferred_element_type=F32)
        l_new = alpha * l_prev + pv[LANES:LANES + 1, :]
        acc_scr[...] = alpha * acc_scr[...] + pv[:LANES, :]
        return m_new, l_new

    acc_scr[...] = jnp.zeros_like(acc_scr)
    m0 = jnp.full((1, 2 * BQ), -jnp.inf, F32)
    l0 = jnp.zeros((1, 2 * BQ), F32)

    @pl.when(i == 0)
    def _():
        first_max[...] = scores(2, 0)

    def pair(j, carry):
        max1, m, l = carry
        max0 = scores(0, j + 1)
        m, l = update(1, j, max1, m, l)
        max1 = scores(1, j + 2)
        m, l = update(0, j + 1, max0, m, l)
        return max1, m, l

    def first_tile(_):
        m, l = update(2, 0, None, m0, l0, diagonal=True)
        next_first()
        return m, l

    def general(_):
        max1 = scores(1, 1)
        m, l = update(2, 0, first_max[...], m0, l0)
        n_full = i - 1
        carry = lax.fori_loop(
            0, n_full // 8,
            lambda t, c: pair(8 * t + 7, pair(8 * t + 5, pair(8 * t + 3, pair(8 * t + 1, c)))),
            (max1, m, l))
        base = 1 + n_full - n_full % 8
        carry = lax.fori_loop(0, (n_full % 8) // 2, lambda t, c: pair(base + 2 * t, c), carry)

        def odd_tail(carry):
            max1, m, l = carry
            scores(0, i)
            m, l = update(1, i - 1, max1, m, l)
            next_first()
            return update(0, i, None, m, l, diagonal=True)

        def even_tail(carry):
            _, m, l = carry
            next_first()
            return update(1, i, None, m, l, diagonal=True)

        return lax.cond(n_full % 2 == 1, odd_tail, even_tail, carry)

    m, l = lax.cond(i == 0, first_tile, general, 0)

    o_all = acc_scr[...] / l
    o_t = o_all[:, :BQ] - lam * o_all[:, BQ:]
    o_t = o_t * lax.rsqrt(jnp.mean(o_t * o_t, axis=0, keepdims=True) + SUBLN_EPS)
    out_ref[...] = (o_t.T * (sub_ref[...] * (1.0 - lam_init))).astype(out_ref.dtype)


def _diff_attn(qt, k, vt, lam_vecs, subln, lam_init, bsz, seq):
    d, t = qt.shape
    nq = seq // BQ
    heads = d // LANES

    def full(a):
        return pl.BlockSpec(a.shape, lambda b, h, i: (0,) * a.ndim, pipeline_mode=pl.Buffered(1))

    kpos = jnp.arange(BK)[:, None]
    qpos = jnp.arange(2 * BQ)[None, :] % BQ
    bias = jnp.where(kpos <= qpos, 0.0, -jnp.inf).astype(F32)

    return pl.pallas_call(
        functools.partial(_diff_attn_kernel, lam_init),
        grid=(bsz, heads, nq),
        in_specs=[pl.BlockSpec((LANES, BQ), lambda b, h, i: (h, b * nq + i)),
                  pl.BlockSpec((LANES, BQ), lambda b, h, i: (h, b * nq + jnp.minimum(i + 1, nq - 1))),
                  pl.BlockSpec((seq, LANES), lambda b, h, i: (b, h)),
                  pl.BlockSpec((LANES, seq), lambda b, h, i: (h, b)),
                  full(lam_vecs), full(subln), full(bias)],
        out_specs=pl.BlockSpec((BQ, LANES), lambda b, h, i: (b * nq + i, h)),
        out_shape=jax.ShapeDtypeStruct((t, d), BF16),
        scratch_shapes=[pltpu.VMEM((3, BK, 2 * BQ), F32), pltpu.VMEM((1, 2 * BQ), F32),
                        pltpu.VMEM((LANES, 2 * BQ), F32)],
        compiler_params=_params(("arbitrary", "arbitrary", "arbitrary")),
        name="diff_attn",
    )(qt, qt, k, vt, lam_vecs, subln, bias)


def _attn_post_kernel(final, o_ref, gate_ref, x_ref, wout_ref, fin_ref, out_ref):
    gate = gate_ref[...].astype(F32)
    o = o_ref[...].astype(F32) * (gate * _sigmoid(gate))
    out = x_ref[...] + _bdot(o, wout_ref[...])
    if final:
        out = _rms(out, fin_ref[...], NORM_EPS)
    out_ref[...] = out


def _attn_post(o, gate, x, w_out, fin, final):
    t, d = x.shape
    tok = pl.BlockSpec((TM_WIDE, d), lambda i: (i, 0))

    def full(a):
        return pl.BlockSpec(a.shape, lambda i: (0,) * a.ndim)

    return pl.pallas_call(
        functools.partial(_attn_post_kernel, final),
        grid=(t // TM_WIDE,),
        in_specs=[tok, tok, tok, full(w_out), full(fin)],
        out_specs=tok,
        out_shape=jax.ShapeDtypeStruct((t, d), F32),
        compiler_params=_params(("arbitrary",)),
        name="attn_post",
    )(o, gate, x, w_out, fin)


def _rope_tables():
    lane = jnp.arange(LANES) % B_QK
    half = ROT_DIM // 2
    inv_freq = ROPE_THETA ** (-jnp.arange(0, ROT_DIM, 2, dtype=F32) / ROT_DIM)
    freq = jnp.where(lane < ROT_DIM, inv_freq[lane % half], 0.0)
    lo = jnp.where(lane < half, -1.0, 0.0)
    hi = jnp.where((lane >= half) & (lane < ROT_DIM), 1.0, 0.0)
    return jnp.zeros((8, LANES), F32).at[0].set(freq).at[1].set(lo).at[2].set(hi)


def kernel(x, positions, a_norm, a_mu, a_w_in, a_w0, a_w1, a_w2, a_a0, a_a1, a_a2, a_v0, a_v1, a_v2, a_k_k, a_k_a, a_r_k, a_ln_w, a_ln_b, a_w_out, kv_norm, w_kv, b_norm, b_w_in, b_lq1, b_lk1, b_lq2, b_lk2, b_subln, b_w_out, final_norm):
    bsz, seq, d = x.shape
    t = bsz * seq
    seg = (jnp.arange(d)[:, None] // A_HEAD == jnp.arange(LANES)[None, :]).astype(BF16)
    seg_t = seg.T

    v_first = None
    for layer in range(N_A):
        zero = jnp.zeros((d,), F32)
        v0 = a_v0[layer - 1] if layer > 0 else zero
        vecs = jnp.stack([*a_mu[layer], a_norm[layer], a_w0[layer], a_a0[layer], v0, a_k_k[layer],
                          a_k_a[layer], a_r_k[layer].reshape(d), a_ln_w[layer], a_ln_b[layer], zero])
        vres = None
        if layer > 0:
            vres = (a_v1[layer - 1].astype(BF16), a_v2[layer - 1].astype(BF16), v_first)
        r, lw, k, v, kk, b, g = _rwkv_pre(
            x, vecs, a_w_in[layer].astype(BF16), a_w1[layer].astype(BF16), a_w2[layer].astype(BF16),
            a_a1[layer].astype(BF16), a_a2[layer].astype(BF16), seg, seg_t, vres)
        if layer == 0:
            v_first = v
        y = _wkv(r, lw, k, v, kk, b)
        flat = lambda a: a.reshape(t, d)
        x = _rwkv_post(flat(y), flat(r), flat(k), flat(v), flat(g), flat(x), vecs,
                       a_w_out[layer].astype(BF16), seg, seg_t).reshape(bsz, seq, d)

    x = x.reshape(t, d)
    pos = jnp.broadcast_to(positions.reshape(t, 1).astype(F32), (t, LANES))
    freq = _rope_tables()
    k_sh, vt_sh, *tables = _proj_rope(x, kv_norm.reshape(1, d), w_kv.astype(BF16), (pos, freq), 1.0,
                                      False, True, BF16)
    n_b = b_norm.shape[0]
    for j in range(n_b):
        layer = N_A + j
        lam_init = 0.8 - 0.6 * math.exp(-0.3 * layer)
        qt, gate = _proj_rope(x, b_norm[j].reshape(1, d), b_w_in[j].astype(BF16), tables,
                              B_QK ** -0.5 * math.log2(math.e), True, False, BF16)
        lam_vecs = jnp.zeros((8, LANES), F32).at[:4, :B_QK].set(
            jnp.stack([b_lq1[j], b_lk1[j], b_lq2[j], b_lk2[j]]))
        o = _diff_attn(qt, k_sh, vt_sh, lam_vecs, b_subln[j].reshape(1, B_V), lam_init, bsz, seq)
        x = _attn_post(o, gate, x, b_w_out[j].astype(BF16), final_norm.reshape(1, d), j == n_b - 1)
    return x.reshape(bsz, seq, d)
```

```python
import functools
import math

import jax
import jax.numpy as jnp
from jax import lax
from jax.experimental import pallas as pl
from jax.experimental.pallas import tpu as pltpu

F32 = jnp.float32
BF16 = jnp.bfloat16

A_HEAD = 64
B_QK = 64
B_V = 128
ROT_DIM = B_QK // 4
ROPE_THETA = 500000.0
NORM_EPS = 1e-6
SUBLN_EPS = 1e-5
GN_EPS = 64e-5
N_A = 2

LANES = 128
SUBLANES = 8
VMEM_LIMIT = 48 * 1024 * 1024

CHUNK = 64
WKV_CHUNKS = 4
TM = 256
TM_WIDE = 512
BQ = 512
BK = 512
ONES_ROWS = 16

_MU, _NORM, _W0, _A0, _V0, _KK, _KA, _RK, _LNW, _LNB = 0, 6, 7, 8, 9, 10, 11, 12, 13, 14


def _params(sem):
    return pltpu.CompilerParams(dimension_semantics=sem, vmem_limit_bytes=VMEM_LIMIT)


def _bdot(a, b):
    return jnp.dot(a.astype(BF16), b.astype(BF16), preferred_element_type=F32)


def _nt(a, b):
    return lax.dot_general(a.astype(BF16), b.astype(BF16), (((1,), (1,)), ((), ())),
                           preferred_element_type=F32)


def _tn(a, b):
    return lax.dot_general(a.astype(BF16), b.astype(BF16), (((0,), (0,)), ((), ())),
                           preferred_element_type=F32)


def _dot_hi_lo(x, m):
    hi = x.astype(BF16)
    lo = (x - hi.astype(F32)).astype(BF16)
    return (jnp.dot(hi, m, preferred_element_type=F32) + jnp.dot(lo, m, preferred_element_type=F32))


def _rms(x, g, eps):
    return x * lax.rsqrt(jnp.mean(x * x, axis=-1, keepdims=True) + eps) * g


def _sigmoid(x):
    return 1.0 / (1.0 + jnp.exp(-x))


def _rwkv_pre_kernel(has_vres, *refs):
    if has_vres:
        (x_ref, vec_ref, win_ref, w1_ref, w2_ref, a1_ref, a2_ref, e_ref, et_ref,
         v1_ref, v2_ref, vf_ref,
         r_out, lw_out, k_out, v_out, kk_out, b_out, g_out, carry) = refs
    else:
        (x_ref, vec_ref, win_ref, w1_ref, w2_ref, a1_ref, a2_ref, e_ref, et_ref,
         r_out, lw_out, k_out, v_out, kk_out, b_out, g_out, carry) = refs

    def vec(i):
        return vec_ref[i:i + 1, :]

    @pl.when(pl.program_id(1) == 0)
    def _():
        carry[...] = jnp.zeros_like(carry)

    xn = _rms(x_ref[...], vec(_NORM), NORM_EPS)
    tm = xn.shape[0]
    rolled = pltpu.roll(xn, 1, axis=0)
    row = lax.broadcasted_iota(jnp.int32, (tm, 1), 0)
    prev = jnp.where(row == 0, carry[7:8, :], rolled)
    carry[...] = xn[tm - 8:, :]
    xx = prev - xn

    def mix(p):
        return xn + xx * vec(_MU + p)

    r = _bdot(mix(0), win_ref[0])
    k = _bdot(mix(1), win_ref[1])
    xm_v = mix(2)
    v = _bdot(xm_v, win_ref[2])
    g_out[...] = _bdot(mix(3), win_ref[3]).astype(g_out.dtype)

    wl = vec(_W0) + _bdot(jnp.tanh(_bdot(mix(4), w1_ref[...])), w2_ref[...])
    lw_out[...] = -math.exp(-0.5) * _sigmoid(wl)
    lr = _sigmoid(vec(_A0) + _bdot(_bdot(mix(5), a1_ref[...]), a2_ref[...]))
    if has_vres:
        gate_v = _sigmoid(vec(_V0) + _bdot(_bdot(xm_v, v1_ref[...]), v2_ref[...]))
        v = v + (vf_ref[...].astype(F32) - v) * gate_v

    kk = k * vec(_KK)
    ss = _bdot(kk * kk, e_ref[...])
    inv = 1.0 / jnp.maximum(jnp.sqrt(ss), 1e-12)
    kk = kk * _dot_hi_lo(inv, et_ref[...])
    r_out[...] = r.astype(r_out.dtype)
    v_out[...] = v.astype(v_out.dtype)
    kk_out[...] = kk.astype(kk_out.dtype)
    b_out[...] = (kk * lr).astype(b_out.dtype)
    k_out[...] = (k * (1.0 + (lr - 1.0) * vec(_KA))).astype(k_out.dtype)


def _rwkv_pre(x, vecs, w_in, w1, w2, a1, a2, seg, seg_t, vres):
    bsz, seq, d = x.shape
    tok = pl.BlockSpec((None, TM, d), lambda b, j: (b, j, 0))

    def full(a):
        return pl.BlockSpec(a.shape, lambda b, j: (0,) * a.ndim, pipeline_mode=pl.Buffered(1))

    args = [x, vecs, w_in, w1, w2, a1, a2, seg, seg_t]
    specs = [tok] + [full(a) for a in args[1:]]
    if vres is not None:
        v1, v2, v_first = vres
        args += [v1, v2, v_first]
        specs += [full(v1), full(v2), tok]
    dtypes = [BF16, F32, BF16, BF16, BF16, BF16, BF16]
    return pl.pallas_call(
        functools.partial(_rwkv_pre_kernel, vres is not None),
        grid=(bsz, seq // TM),
        in_specs=specs,
        out_specs=[tok] * 7,
        out_shape=[jax.ShapeDtypeStruct((bsz, seq, d), dt) for dt in dtypes],
        scratch_shapes=[pltpu.VMEM((8, d), F32)],
        compiler_params=_params(("arbitrary", "arbitrary")),
        name="rwkv_pre",
    )(*args)


def _stack(x, lane_head):
    x = x.astype(BF16)
    zero = jnp.zeros_like(x)
    return jnp.concatenate([jnp.where(lane_head == 0, x, zero),
                            jnp.where(lane_head == 1, x, zero)], axis=0)


def _wkv_prepare(ops, tri):
    units = range(len(ops))
    lane = lax.broadcasted_iota(jnp.int32, (CHUNK, LANES), 1)
    lane_head = lane // A_HEAD

    def cumsum(lw):
        h1 = lw.astype(BF16)
        r1 = lw - h1.astype(F32)
        h2 = r1.astype(BF16)
        h3 = (r1 - h2.astype(F32)).astype(BF16)
        return (jnp.dot(tri, h1, preferred_element_type=F32) + jnp.dot(tri, h2, preferred_element_type=F32)
                + jnp.dot(tri, h3, preferred_element_type=F32))

    cls = [cumsum(ops[u][1]) for u in units]

    def prep(op, cl):
        r, lw, k, v, kk, b = (t.astype(F32) for t in op)
        e_neg = jnp.exp(-cl)
        e_end = jnp.exp(cl[CHUNK - 1:CHUNK, :])
        bt = b * e_neg
        kt = k * e_neg
        a_n = (-kk * jnp.exp(cl - lw)).astype(BF16)
        r_n = (r * jnp.exp(cl)).astype(BF16)
        return dict(
            e_end=e_end, a_n=a_n, r_n=r_n, v_n=v.astype(BF16), v_s=_stack(v, lane_head),
            ar=jnp.concatenate([a_n, r_n], axis=0),
            bk=jnp.concatenate([_stack(bt, lane_head), _stack(kt, lane_head)], axis=0),
            bkh=jnp.concatenate([bt * e_end, kt * e_end], axis=0).astype(BF16))

    pre = [prep(ops[u], cls[u]) for u in units]
    sc = [_nt(pre[u]["ar"], pre[u]["bk"]) for u in units]

    row = lax.broadcasted_iota(jnp.int32, (CHUNK, LANES), 0)
    col = lane % A_HEAD
    strict = col < row
    incl = col <= row
    eye = (col == row).astype(F32)
    a_ab = [jnp.where(strict, sc[u][:CHUNK, :LANES], 0.0) for u in units]
    a_ak = [jnp.where(strict, sc[u][:CHUNK, LANES:], 0.0).astype(BF16) for u in units]
    a_r = [jnp.concatenate([jnp.where(incl, sc[u][CHUNK:, :LANES], 0.0),
                            jnp.where(incl, sc[u][CHUNK:, LANES:], 0.0)], axis=1).astype(BF16)
           for u in units]

    t_inv = [eye + a_ab[u] for u in units]
    pw = [_bdot(a_ab[u], _stack(a_ab[u], lane_head)) for u in units]
    for _ in range(int(math.log2(CHUNK)) - 2):
        both = [_bdot(jnp.concatenate([t_inv[u], pw[u]], axis=0), _stack(pw[u], lane_head))
                for u in units]
        t_inv = [t_inv[u] + both[u][:CHUNK] for u in units]
        pw = [both[u][CHUNK:] for u in units]
    t_inv = [t_inv[u] + _bdot(t_inv[u], _stack(pw[u], lane_head)) for u in units]

    akv = [_bdot(a_ak[u], pre[u]["v_s"]) for u in units]
    return [dict(pre[u], t_inv=t_inv[u].astype(BF16), akv=akv[u], a_r=a_r[u]) for u in units]


def _wkv_advance(prep, states):
    units = range(len(prep))
    lane_head = lax.broadcasted_iota(jnp.int32, (CHUNK, LANES), 1) // A_HEAD
    st = [states[u].astype(BF16) for u in units]
    ar_st = [_nt(prep[u]["ar"], st[u]) for u in units]
    rhs = [ar_st[u][:CHUNK] + prep[u]["akv"] for u in units]
    us = [jnp.dot(prep[u]["t_inv"], _stack(rhs[u], lane_head), preferred_element_type=F32)
          for u in units]
    uv_n = [jnp.concatenate([us[u].astype(BF16), prep[u]["v_n"]], axis=0) for u in units]
    same_head = (lax.broadcasted_iota(jnp.int32, (LANES, LANES), 0) // A_HEAD
                 == lax.broadcasted_iota(jnp.int32, (LANES, LANES), 1) // A_HEAD)
    new_states = [states[u] * prep[u]["e_end"] + jnp.where(same_head, _tn(uv_n[u], prep[u]["bkh"]), 0.0)
                  for u in units]
    ys = [ar_st[u][CHUNK:]
          + _bdot(prep[u]["a_r"], jnp.concatenate([_stack(us[u], lane_head), prep[u]["v_s"]], axis=0))
          for u in units]
    return ys, new_states


def _wkv_kernel(r_ref, lw_ref, k_ref, v_ref, kk_ref, b_ref, y_ref, state):
    @pl.when(pl.program_id(0) == 0)
    def _():
        state[...] = jnp.zeros_like(state)

    tri = (lax.broadcasted_iota(jnp.int32, (CHUNK, CHUNK), 1)
           <= lax.broadcasted_iota(jnp.int32, (CHUNK, CHUNK), 0)).astype(BF16)
    bsz, _, d = r_ref.shape
    pairs = [(bi, slice(p * LANES, (p + 1) * LANES)) for bi in range(bsz) for p in range(d // LANES)]
    rows = [slice(c * CHUNK, (c + 1) * CHUNK) for c in range(WKV_CHUNKS)]
    prep = _wkv_prepare([tuple(ref[bi, rs, sl] for ref in (r_ref, lw_ref, k_ref, v_ref, kk_ref, b_ref))
                         for rs in rows for bi, sl in pairs], tri)
    states = [state[u] for u in range(len(pairs))]
    for c, rs in enumerate(rows):
        ys, states = _wkv_advance(prep[c * len(pairs):(c + 1) * len(pairs)], states)
        for u, (bi, sl) in enumerate(pairs):
            y_ref[bi, rs, sl] = ys[u].astype(y_ref.dtype)
    for u in range(len(pairs)):
        state[u] = states[u]


def _wkv(r, lw, k, v, kk, b):
    bsz, seq, d = r.shape
    tok = pl.BlockSpec((bsz, WKV_CHUNKS * CHUNK, d), lambda j: (0, j, 0))
    return pl.pallas_call(
        _wkv_kernel,
        grid=(seq // (WKV_CHUNKS * CHUNK),),
        in_specs=[tok] * 6,
        out_specs=tok,
        out_shape=jax.ShapeDtypeStruct((bsz, seq, d), BF16),
        scratch_shapes=[pltpu.VMEM((bsz * d // LANES, LANES, LANES), F32)],
        compiler_params=_params(("arbitrary",)),
        name="wkv",
    )(r, lw, k, v, kk, b)


def _rwkv_post_kernel(y_ref, r_ref, k_ref, v_ref, g_ref, x_ref, vec_ref, wout_ref, e_ref, et_ref,
                      out_ref):
    def vec(i):
        return vec_ref[i:i + 1, :]

    seg = e_ref[...]
    seg_t = et_ref[...]

    def head_sum(t):
        return _bdot(t, seg)

    def spread(t):
        return _dot_hi_lo(t, seg_t)

    y = y_ref[...].astype(F32)
    yc = y - spread(head_sum(y) * (1.0 / A_HEAD))
    rstd = lax.rsqrt(head_sum(yc * yc) * (1.0 / A_HEAD) + GN_EPS)
    yn = yc * spread(rstd) * vec(_LNW) + vec(_LNB)
    rk = r_ref[...].astype(F32) * k_ref[...].astype(F32) * vec(_RK)
    bonus = spread(head_sum(rk)) * v_ref[...].astype(F32)
    g = g_ref[...].astype(F32)
    o = (yn + bonus) * (g * _sigmoid(g))
    out_ref[...] = x_ref[...] + _bdot(o, wout_ref[...])


def _rwkv_post(y, r, k, v, g, x, vecs, w_out, seg, seg_t):
    t, d = x.shape
    tok = pl.BlockSpec((TM_WIDE, d), lambda i: (i, 0))

    def full(a):
        return pl.BlockSpec(a.shape, lambda i: (0,) * a.ndim)

    return pl.pallas_call(
        _rwkv_post_kernel,
        grid=(t // TM_WIDE,),
        in_specs=[tok] * 6 + [full(vecs), full(w_out), full(seg), full(seg_t)],
        out_specs=tok,
        out_shape=jax.ShapeDtypeStruct((t, d), F32),
        compiler_params=_params(("arbitrary",)),
        name="rwkv_post",
    )(y, r, k, v, g, x, vecs, w_out, seg, seg_t)


def _proj_rope_kernel(scale, rot_t, lin_t, make_tables, x_ref, g_ref, w_ref, *refs):
    xn = _rms(x_ref[...], g_ref[...], NORM_EPS)
    proj = _bdot(xn, w_ref[...])
    d = x_ref.shape[1]
    if make_tables:
        pos_ref, freq_ref, rot_ref, lin_ref, cos_ref, lo_ref, hi_ref = refs
        ang = pos_ref[...] * freq_ref[0:1, :]
        cos = jnp.cos(ang)
        sin = jnp.sin(ang)
        sin_lo = sin * freq_ref[1:2, :]
        sin_hi = sin * freq_ref[2:3, :]
        cos_ref[...] = cos
        lo_ref[...] = sin_lo
        hi_ref[...] = sin_hi
    else:
        cos_ref, lo_ref, hi_ref, rot_ref, lin_ref = refs
        cos, sin_lo, sin_hi = cos_ref[...], lo_ref[...], hi_ref[...]
    if scale != 1.0:
        cos, sin_lo, sin_hi = cos * scale, sin_lo * scale, sin_hi * scale
    half = ROT_DIM // 2
    for c in range(d // LANES):
        cols = slice(c * LANES, (c + 1) * LANES)
        t = proj[:, cols]
        rot = (t * cos + pltpu.roll(t, LANES - half, axis=1) * sin_lo
               + pltpu.roll(t, half, axis=1) * sin_hi)
        if rot_t:
            rot_ref[cols, :] = rot.T.astype(rot_ref.dtype)
        else:
            rot_ref[:, cols] = rot.astype(rot_ref.dtype)
        if lin_t:
            lin_ref[cols, :] = proj[:, d + c * LANES:d + (c + 1) * LANES].T.astype(lin_ref.dtype)
    if not lin_t:
        lin_ref[...] = proj[:, d:].astype(lin_ref.dtype)


def _proj_rope(x, g, w, rope, scale, rot_t, lin_t, lin_dtype):
    t, d = x.shape
    make_tables = len(rope) == 2
    tok = pl.BlockSpec((TM_WIDE, d), lambda i: (i, 0))
    tok_t = pl.BlockSpec((d, TM_WIDE), lambda i: (0, i))
    table = pl.BlockSpec((TM_WIDE, LANES), lambda i: (i, 0))

    def full(a):
        return pl.BlockSpec(a.shape, lambda i: (0,) * a.ndim)

    def shape(transposed, dtype):
        return jax.ShapeDtypeStruct((d, t) if transposed else (t, d), dtype)

    if make_tables:
        rope_specs = [table, full(rope[1])]
    else:
        rope_specs = [table] * 3
    out_specs = [tok_t if rot_t else tok, tok_t if lin_t else tok]
    out_shape = [shape(rot_t, BF16), shape(lin_t, lin_dtype)]
    if make_tables:
        out_specs += [table] * 3
        out_shape += [jax.ShapeDtypeStruct((t, LANES), F32)] * 3
    return pl.pallas_call(
        functools.partial(_proj_rope_kernel, scale, rot_t, lin_t, make_tables),
        grid=(t // TM_WIDE,),
        in_specs=[tok, full(g), full(w)] + rope_specs,
        out_specs=out_specs,
        out_shape=out_shape,
        compiler_params=_params(("arbitrary",)),
        name="proj_rope",
    )(x, g, w, *rope)


def _diff_attn_kernel(lam_init, qt_ref, qn_ref, k_ref, vt_ref, lam_ref, sub_ref, bias_ref, out_ref,
                      s_scr, first_max, acc_scr):
    i = pl.program_id(2)

    def stacked(qt):
        row_map = lax.broadcasted_iota(jnp.int32, qt.shape, 0) // B_QK
        zero = jnp.zeros_like(qt)
        return jnp.concatenate([jnp.where(row_map == 0, qt, zero), jnp.where(row_map == 1, qt, zero)],
                               axis=1)

    qs = stacked(qt_ref[...])
    lam = (jnp.exp(jnp.sum(lam_ref[0:1, :] * lam_ref[1:2, :], axis=1, keepdims=True))
           - jnp.exp(jnp.sum(lam_ref[2:3, :] * lam_ref[3:4, :], axis=1, keepdims=True)) + lam_init)

    def scores(slot, j, q=qs):
        kb = k_ref[pl.ds(pl.multiple_of(j * BK, BK), BK), :]
        s = jnp.dot(kb, q, preferred_element_type=F32)
        s_scr[slot] = s
        return jnp.max(s, axis=0, keepdims=True)

    def next_first():
        first_max[0:1, :] = scores(2, 0, stacked(qn_ref[...]))

    def update(slot, j, s_max, m_prev, l_prev, diagonal=False):
        s = s_scr[slot]
        if diagonal:
            s = s + bias_ref[...]
            s_max = jnp.max(s, axis=0, keepdims=True)
        m_new = jnp.maximum(m_prev, s_max)
        p = jnp.exp2(s - m_new)
        alpha = jnp.exp2(m_prev - m_new)
        vb = jnp.concatenate([vt_ref[:, pl.ds(pl.multiple_of(j * BK, BK), BK)],
                              jnp.ones((ONES_ROWS, BK), BF16)], axis=0)
        pv = jnp.dot(vb, p.astype(BF16), preferred_element_type=F32)
        l_new = alpha * l_prev + pv[LANES:LANES + 1, :]
        acc_scr[...] = alpha * acc_scr[...] + pv[:LANES, :]
        return m_new, l_new

    acc_scr[...] = jnp.zeros_like(acc_scr)
    m0 = jnp.full((1, 2 * BQ), -jnp.inf, F32)
    l0 = jnp.zeros((1, 2 * BQ), F32)

    @pl.when(i == 0)
    def _():
        first_max[0:1, :] = scores(2, 0)

    def pair(j, carry):
        max1, m, l = carry
        max0 = scores(0, j + 1)
        m, l = update(1, j, max1, m, l)
        max1 = scores(1, j + 2)
        m, l = update(0, j + 1, max0, m, l)
        return max1, m, l

    def first_tile(_):
        m, l = update(2, 0, None, m0, l0, diagonal=True)
        next_first()
        return m, l

    def general(_):
        max1 = scores(1, 1)
        m, l = update(2, 0, first_max[0:1, :], m0, l0)
        n_full = i - 1
        carry = lax.fori_loop(
            0, n_full // 8,
            lambda t, c: pair(8 * t + 7, pair(8 * t + 5, pair(8 * t + 3, pair(8 * t + 1, c)))),
            (max1, m, l))
        base = 1 + n_full - n_full % 8
        carry = lax.fori_loop(0, (n_full % 8) // 2, lambda t, c: pair(base + 2 * t, c), carry)

        def odd_tail(carry):
            max1, m, l = carry
            scores(0, i)
            m, l = update(1, i - 1, max1, m, l)
            next_first()
            return update(0, i, None, m, l, diagonal=True)

        def even_tail(carry):
            _, m, l = carry
            next_first()
            return update(1, i, None, m, l, diagonal=True)

        return lax.cond(n_full % 2 == 1, odd_tail, even_tail, carry)

    m, l = lax.cond(i == 0, first_tile, general, 0)

    o_all = acc_scr[...] / l
    o_t = o_all[:, :BQ] - lam * o_all[:, BQ:]
    o_t = o_t * lax.rsqrt(jnp.mean(o_t * o_t, axis=0, keepdims=True) + SUBLN_EPS)
    out_ref[...] = (o_t.T * (sub_ref[...] * (1.0 - lam_init))).astype(out_ref.dtype)


def _diff_attn(qt, k, vt, lam_vecs, subln, lam_init, bsz, seq):
    d, t = qt.shape
    nq = seq // BQ
    heads = d // LANES

    def full(a):
        return pl.BlockSpec(a.shape, lambda b, h, i: (0,) * a.ndim, pipeline_mode=pl.Buffered(1))

    kpos = jnp.arange(BK)[:, None]
    qpos = jnp.arange(2 * BQ)[None, :] % BQ
    bias = jnp.where(kpos <= qpos, 0.0, -jnp.inf).astype(F32)

    return pl.pallas_call(
        functools.partial(_diff_attn_kernel, lam_init),
        grid=(bsz, heads, nq),
        in_specs=[pl.BlockSpec((LANES, BQ), lambda b, h, i: (h, b * nq + i)),
                  pl.BlockSpec((LANES, BQ), lambda b, h, i: (h, b * nq + jnp.minimum(i + 1, nq - 1))),
                  pl.BlockSpec((seq, LANES), lambda b, h, i: (b, h)),
                  pl.BlockSpec((LANES, seq), lambda b, h, i: (h, b)),
                  full(lam_vecs), full(subln), full(bias)],
        out_specs=pl.BlockSpec((BQ, LANES), lambda b, h, i: (b * nq + i, h)),
        out_shape=jax.ShapeDtypeStruct((t, d), BF16),
        scratch_shapes=[pltpu.VMEM((3, BK, 2 * BQ), F32), pltpu.VMEM((SUBLANES, 2 * BQ), F32),
                        pltpu.VMEM((LANES, 2 * BQ), F32)],
        compiler_params=_params(("arbitrary", "arbitrary", "arbitrary")),
        name="diff_attn",
    )(qt, qt, k, vt, lam_vecs, subln, bias)


def _attn_post_kernel(final, o_ref, gate_ref, x_ref, wout_ref, fin_ref, out_ref):
    gate = gate_ref[...].astype(F32)
    o = o_ref[...].astype(F32) * (gate * _sigmoid(gate))
    out = x_ref[...] + _bdot(o, wout_ref[...])
    if final:
        out = _rms(out, fin_ref[...], NORM_EPS)
    out_ref[...] = out


def _attn_post(o, gate, x, w_out, fin, final):
    t, d = x.shape
    tok = pl.BlockSpec((TM_WIDE, d), lambda i: (i, 0))

    def full(a):
        return pl.BlockSpec(a.shape, lambda i: (0,) * a.ndim)

    return pl.pallas_call(
        functools.partial(_attn_post_kernel, final),
        grid=(t // TM_WIDE,),
        in_specs=[tok, tok, tok, full(w_out), full(fin)],
        out_specs=tok,
        out_shape=jax.ShapeDtypeStruct((t, d), F32),
        compiler_params=_params(("arbitrary",)),
        name="attn_post",
    )(o, gate, x, w_out, fin)


def _rope_tables():
    lane = jnp.arange(LANES) % B_QK
    half = ROT_DIM // 2
    inv_freq = ROPE_THETA ** (-jnp.arange(0, ROT_DIM, 2, dtype=F32) / ROT_DIM)
    freq = jnp.where(lane < ROT_DIM, inv_freq[lane % half], 0.0)
    lo = jnp.where(lane < half, -1.0, 0.0)
    hi = jnp.where((lane >= half) & (lane < ROT_DIM), 1.0, 0.0)
    return jnp.zeros((8, LANES), F32).at[0].set(freq).at[1].set(lo).at[2].set(hi)


def kernel(x, positions, a_norm, a_mu, a_w_in, a_w0, a_w1, a_w2, a_a0, a_a1, a_a2, a_v0, a_v1, a_v2, a_k_k, a_k_a, a_r_k, a_ln_w, a_ln_b, a_w_out, kv_norm, w_kv, b_norm, b_w_in, b_lq1, b_lk1, b_lq2, b_lk2, b_subln, b_w_out, final_norm):
    bsz, seq, d = x.shape
    t = bsz * seq
    seg = (jnp.arange(d)[:, None] // A_HEAD == jnp.arange(LANES)[None, :]).astype(BF16)
    seg_t = seg.T

    v_first = None
    for layer in range(N_A):
        zero = jnp.zeros((d,), F32)
        v0 = a_v0[layer - 1] if layer > 0 else zero
        vecs = jnp.stack([*a_mu[layer], a_norm[layer], a_w0[layer], a_a0[layer], v0, a_k_k[layer],
                          a_k_a[layer], a_r_k[layer].reshape(d), a_ln_w[layer], a_ln_b[layer], zero])
        vres = None
        if layer > 0:
            vres = (a_v1[layer - 1].astype(BF16), a_v2[layer - 1].astype(BF16), v_first)
        r, lw, k, v, kk, b, g = _rwkv_pre(
            x, vecs, a_w_in[layer].astype(BF16), a_w1[layer].astype(BF16), a_w2[layer].astype(BF16),
            a_a1[layer].astype(BF16), a_a2[layer].astype(BF16), seg, seg_t, vres)
        if layer == 0:
            v_first = v
        y = _wkv(r, lw, k, v, kk, b)
        flat = lambda a: a.reshape(t, d)
        x = _rwkv_post(flat(y), flat(r), flat(k), flat(v), flat(g), flat(x), vecs,
                       a_w_out[layer].astype(BF16), seg, seg_t).reshape(bsz, seq, d)

    x = x.reshape(t, d)
    pos = jnp.broadcast_to(positions.reshape(t, 1).astype(F32), (t, LANES))
    freq = _rope_tables()
    k_sh, vt_sh, *tables = _proj_rope(x, kv_norm.reshape(1, d), w_kv.astype(BF16), (pos, freq), 1.0,
                                      False, True, BF16)
    n_b = b_norm.shape[0]
    for j in range(n_b):
        layer = N_A + j
        lam_init = 0.8 - 0.6 * math.exp(-0.3 * layer)
        qt, gate = _proj_rope(x, b_norm[j].reshape(1, d), b_w_in[j].astype(BF16), tables,
                              B_QK ** -0.5 * math.log2(math.e), True, False, BF16)
        lam_vecs = jnp.zeros((8, LANES), F32).at[:4, :B_QK].set(
            jnp.stack([b_lq1[j], b_lk1[j], b_lq2[j], b_lk2[j]]))
        o = _diff_attn(qt, k_sh, vt_sh, lam_vecs, b_subln[j].reshape(1, B_V), lam_init, bsz, seq)
        x = _attn_post(o, gate, x, b_w_out[j].astype(BF16), final_norm.reshape(1, d), j == n_b - 1)
    return x.reshape(bsz, seq, d)
```

```python
import functools
import math

import jax
import jax.numpy as jnp
from jax import lax
from jax.experimental import pallas as pl
from jax.experimental.pallas import tpu as pltpu

F32 = jnp.float32
BF16 = jnp.bfloat16

A_HEAD = 64
B_QK = 64
B_V = 128
ROT_DIM = B_QK // 4
ROPE_THETA = 500000.0
NORM_EPS = 1e-6
SUBLN_EPS = 1e-5
GN_EPS = 64e-5
N_A = 2

LANES = 128
VMEM_LIMIT = 48 * 1024 * 1024

CHUNK = 64
WKV_CHUNKS = 4
TM = 256
TM_WIDE = 512
BQ = 512
BK = 512
ONES_ROWS = 16

_MU, _NORM, _W0, _A0, _V0, _KK, _KA, _RK, _LNW, _LNB = 0, 6, 7, 8, 9, 10, 11, 12, 13, 14


def _params(sem):
    return pltpu.CompilerParams(dimension_semantics=sem, vmem_limit_bytes=VMEM_LIMIT)


def _bdot(a, b):
    return jnp.dot(a.astype(BF16), b.astype(BF16), preferred_element_type=F32)


def _nt(a, b):
    return lax.dot_general(a.astype(BF16), b.astype(BF16), (((1,), (1,)), ((), ())),
                           preferred_element_type=F32)


def _tn(a, b):
    return lax.dot_general(a.astype(BF16), b.astype(BF16), (((0,), (0,)), ((), ())),
                           preferred_element_type=F32)


def _dot_hi_lo(x, m):
    hi = x.astype(BF16)
    lo = (x - hi.astype(F32)).astype(BF16)
    return (jnp.dot(hi, m, preferred_element_type=F32) + jnp.dot(lo, m, preferred_element_type=F32))


def _rms(x, g, eps):
    return x * lax.rsqrt(jnp.mean(x * x, axis=-1, keepdims=True) + eps) * g


def _sigmoid(x):
    return 1.0 / (1.0 + jnp.exp(-x))


def _rwkv_pre_kernel(has_vres, *refs):
    if has_vres:
        (x_ref, vec_ref, win_ref, w1_ref, w2_ref, a1_ref, a2_ref, e_ref, et_ref,
         v1_ref, v2_ref, vf_ref,
         r_out, lw_out, k_out, v_out, kk_out, b_out, g_out, carry) = refs
    else:
        (x_ref, vec_ref, win_ref, w1_ref, w2_ref, a1_ref, a2_ref, e_ref, et_ref,
         r_out, lw_out, k_out, v_out, kk_out, b_out, g_out, carry) = refs

    def vec(i):
        return vec_ref[i:i + 1, :]

    @pl.when(pl.program_id(1) == 0)
    def _():
        carry[...] = jnp.zeros_like(carry)

    xn = _rms(x_ref[...], vec(_NORM), NORM_EPS)
    tm = xn.shape[0]
    rolled = pltpu.roll(xn, 1, axis=0)
    row = lax.broadcasted_iota(jnp.int32, (tm, 1), 0)
    prev = jnp.where(row == 0, carry[7:8, :], rolled)
    carry[...] = xn[tm - 8:, :]
    xx = prev - xn

    def mix(p):
        return xn + xx * vec(_MU + p)

    r = _bdot(mix(0), win_ref[0])
    k = _bdot(mix(1), win_ref[1])
    xm_v = mix(2)
    v = _bdot(xm_v, win_ref[2])
    g_out[...] = _bdot(mix(3), win_ref[3]).astype(g_out.dtype)

    wl = vec(_W0) + _bdot(jnp.tanh(_bdot(mix(4), w1_ref[...])), w2_ref[...])
    lw_out[...] = -math.exp(-0.5) * _sigmoid(wl)
    lr = _sigmoid(vec(_A0) + _bdot(_bdot(mix(5), a1_ref[...]), a2_ref[...]))
    if has_vres:
        gate_v = _sigmoid(vec(_V0) + _bdot(_bdot(xm_v, v1_ref[...]), v2_ref[...]))
        v = v + (vf_ref[...].astype(F32) - v) * gate_v

    kk = k * vec(_KK)
    ss = _bdot(kk * kk, e_ref[...])
    inv = 1.0 / jnp.maximum(jnp.sqrt(ss), 1e-12)
    kk = kk * _dot_hi_lo(inv, et_ref[...])
    r_out[...] = r.astype(r_out.dtype)
    v_out[...] = v.astype(v_out.dtype)
    kk_out[...] = kk.astype(kk_out.dtype)
    b_out[...] = (kk * lr).astype(b_out.dtype)
    k_out[...] = (k * (1.0 + (lr - 1.0) * vec(_KA))).astype(k_out.dtype)


def _rwkv_pre(x, vecs, w_in, w1, w2, a1, a2, seg, seg_t, vres):
    bsz, seq, d = x.shape
    tok = pl.BlockSpec((None, TM, d), lambda b, j: (b, j, 0))

    def full(a):
        return pl.BlockSpec(a.shape, lambda b, j: (0,) * a.ndim, pipeline_mode=pl.Buffered(1))

    args = [x, vecs, w_in, w1, w2, a1, a2, seg, seg_t]
    specs = [tok] + [full(a) for a in args[1:]]
    if vres is not None:
        v1, v2, v_first = vres
        args += [v1, v2, v_first]
        specs += [full(v1), full(v2), tok]
    dtypes = [BF16, F32, BF16, BF16, BF16, BF16, BF16]
    return pl.pallas_call(
        functools.partial(_rwkv_pre_kernel, vres is not None),
        grid=(bsz, seq // TM),
        in_specs=specs,
        out_specs=[tok] * 7,
        out_shape=[jax.ShapeDtypeStruct((bsz, seq, d), dt) for dt in dtypes],
        scratch_shapes=[pltpu.VMEM((8, d), F32)],
        compiler_params=_params(("arbitrary", "arbitrary")),
        name="rwkv_pre",
    )(*args)


def _stack(x, lane_head):
    x = x.astype(BF16)
    zero = jnp.zeros_like(x)
    return jnp.concatenate([jnp.where(lane_head == 0, x, zero),
                            jnp.where(lane_head == 1, x, zero)], axis=0)


def _wkv_prepare(ops, tri):
    units = range(len(ops))
    lane = lax.broadcasted_iota(jnp.int32, (CHUNK, LANES), 1)
    lane_head = lane // A_HEAD

    def cumsum(lw):
        h1 = lw.astype(BF16)
        r1 = lw - h1.astype(F32)
        h2 = r1.astype(BF16)
        h3 = (r1 - h2.astype(F32)).astype(BF16)
        return (jnp.dot(tri, h1, preferred_element_type=F32) + jnp.dot(tri, h2, preferred_element_type=F32)
                + jnp.dot(tri, h3, preferred_element_type=F32))

    cls = [cumsum(ops[u][1]) for u in units]

    def prep(op, cl):
        r, lw, k, v, kk, b = (t.astype(F32) for t in op)
        e_neg = jnp.exp(-cl)
        e_end = jnp.exp(cl[CHUNK - 1:CHUNK, :])
        bt = b * e_neg
        kt = k * e_neg
        a_n = (-kk * jnp.exp(cl - lw)).astype(BF16)
        r_n = (r * jnp.exp(cl)).astype(BF16)
        return dict(
            e_end=e_end, a_n=a_n, r_n=r_n, v_n=v.astype(BF16), v_s=_stack(v, lane_head),
            ar=jnp.concatenate([a_n, r_n], axis=0),
            bk=jnp.concatenate([_stack(bt, lane_head), _stack(kt, lane_head)], axis=0),
            bkh=jnp.concatenate([bt * e_end, kt * e_end], axis=0).astype(BF16))

    pre = [prep(ops[u], cls[u]) for u in units]
    sc = [_nt(pre[u]["ar"], pre[u]["bk"]) for u in units]

    row = lax.broadcasted_iota(jnp.int32, (CHUNK, LANES), 0)
    col = lane % A_HEAD
    strict = col < row
    incl = col <= row
    eye = (col == row).astype(F32)
    a_ab = [jnp.where(strict, sc[u][:CHUNK, :LANES], 0.0) for u in units]
    a_ak = [jnp.where(strict, sc[u][:CHUNK, LANES:], 0.0).astype(BF16) for u in units]
    a_r = [jnp.concatenate([jnp.where(incl, sc[u][CHUNK:, :LANES], 0.0),
                            jnp.where(incl, sc[u][CHUNK:, LANES:], 0.0)], axis=1).astype(BF16)
           for u in units]

    t_inv = [eye + a_ab[u] for u in units]
    pw = [_bdot(a_ab[u], _stack(a_ab[u], lane_head)) for u in units]
    for _ in range(int(math.log2(CHUNK)) - 2):
        both = [_bdot(jnp.concatenate([t_inv[u], pw[u]], axis=0), _stack(pw[u], lane_head))
                for u in units]
        t_inv = [t_inv[u] + both[u][:CHUNK] for u in units]
        pw = [both[u][CHUNK:] for u in units]
    t_inv = [t_inv[u] + _bdot(t_inv[u], _stack(pw[u], lane_head)) for u in units]

    akv = [_bdot(a_ak[u], pre[u]["v_s"]) for u in units]
    return [dict(pre[u], t_inv=t_inv[u].astype(BF16), akv=akv[u], a_r=a_r[u]) for u in units]


def _wkv_advance(prep, states):
    units = range(len(prep))
    lane_head = lax.broadcasted_iota(jnp.int32, (CHUNK, LANES), 1) // A_HEAD
    st = [states[u].astype(BF16) for u in units]
    ar_st = [_nt(prep[u]["ar"], st[u]) for u in units]
    rhs = [ar_st[u][:CHUNK] + prep[u]["akv"] for u in units]
    us = [jnp.dot(prep[u]["t_inv"], _stack(rhs[u], lane_head), preferred_element_type=F32)
          for u in units]
    uv_n = [jnp.concatenate([us[u].astype(BF16), prep[u]["v_n"]], axis=0) for u in units]
    same_head = (lax.broadcasted_iota(jnp.int32, (LANES, LANES), 0) // A_HEAD
                 == lax.broadcasted_iota(jnp.int32, (LANES, LANES), 1) // A_HEAD)
    new_states = [states[u] * prep[u]["e_end"] + jnp.where(same_head, _tn(uv_n[u], prep[u]["bkh"]), 0.0)
                  for u in units]
    ys = [ar_st[u][CHUNK:]
          + _bdot(prep[u]["a_r"], jnp.concatenate([_stack(us[u], lane_head), prep[u]["v_s"]], axis=0))
          for u in units]
    return ys, new_states


def _wkv_kernel(r_ref, lw_ref, k_ref, v_ref, kk_ref, b_ref, y_ref, state):
    @pl.when(pl.program_id(0) == 0)
    def _():
        state[...] = jnp.zeros_like(state)

    tri = (lax.broadcasted_iota(jnp.int32, (CHUNK, CHUNK), 1)
           <= lax.broadcasted_iota(jnp.int32, (CHUNK, CHUNK), 0)).astype(BF16)
    bsz, _, d = r_ref.shape
    pairs = [(bi, slice(p * LANES, (p + 1) * LANES)) for bi in range(bsz) for p in range(d // LANES)]
    rows = [slice(c * CHUNK, (c + 1) * CHUNK) for c in range(WKV_CHUNKS)]
    prep = _wkv_prepare([tuple(ref[bi, rs, sl] for ref in (r_ref, lw_ref, k_ref, v_ref, kk_ref, b_ref))
                         for rs in rows for bi, sl in pairs], tri)
    states = [state[u] for u in range(len(pairs))]
    for c, rs in enumerate(rows):
        ys, states = _wkv_advance(prep[c * len(pairs):(c + 1) * len(pairs)], states)
        for u, (bi, sl) in enumerate(pairs):
            y_ref[bi, rs, sl] = ys[u].astype(y_ref.dtype)
    for u in range(len(pairs)):
        state[u] = states[u]


def _wkv(r, lw, k, v, kk, b):
    bsz, seq, d = r.shape
    tok = pl.BlockSpec((bsz, WKV_CHUNKS * CHUNK, d), lambda j: (0, j, 0))
    return pl.pallas_call(
        _wkv_kernel,
        grid=(seq // (WKV_CHUNKS * CHUNK),),
        in_specs=[tok] * 6,
        out_specs=tok,
        out_shape=jax.ShapeDtypeStruct((bsz, seq, d), BF16),
        scratch_shapes=[pltpu.VMEM((bsz * d // LANES, LANES, LANES), F32)],
        compiler_params=_params(("arbitrary",)),
        name="wkv",
    )(r, lw, k, v, kk, b)


def _rwkv_post_kernel(y_ref, r_ref, k_ref, v_ref, g_ref, x_ref, vec_ref, wout_ref, e_ref, et_ref,
                      out_ref):
    def vec(i):
        return vec_ref[i:i + 1, :]

    seg = e_ref[...]
    seg_t = et_ref[...]

    def head_sum(t):
        return _bdot(t, seg)

    def spread(t):
        return _dot_hi_lo(t, seg_t)

    y = y_ref[...].astype(F32)
    yc = y - spread(head_sum(y) * (1.0 / A_HEAD))
    rstd = lax.rsqrt(head_sum(yc * yc) * (1.0 / A_HEAD) + GN_EPS)
    yn = yc * spread(rstd) * vec(_LNW) + vec(_LNB)
    rk = r_ref[...].astype(F32) * k_ref[...].astype(F32) * vec(_RK)
    bonus = spread(head_sum(rk)) * v_ref[...].astype(F32)
    g = g_ref[...].astype(F32)
    o = (yn + bonus) * (g * _sigmoid(g))
    out_ref[...] = x_ref[...] + _bdot(o, wout_ref[...])


def _rwkv_post(y, r, k, v, g, x, vecs, w_out, seg, seg_t):
    t, d = x.shape
    tok = pl.BlockSpec((TM_WIDE, d), lambda i: (i, 0))

    def full(a):
        return pl.BlockSpec(a.shape, lambda i: (0,) * a.ndim)

    return pl.pallas_call(
        _rwkv_post_kernel,
        grid=(t // TM_WIDE,),
        in_specs=[tok] * 6 + [full(vecs), full(w_out), full(seg), full(seg_t)],
        out_specs=tok,
        out_shape=jax.ShapeDtypeStruct((t, d), F32),
        compiler_params=_params(("arbitrary",)),
        name="rwkv_post",
    )(y, r, k, v, g, x, vecs, w_out, seg, seg_t)


def _proj_rope_kernel(scale, rot_t, lin_t, make_tables, x_ref, g_ref, w_ref, *refs):
    xn = _rms(x_ref[...], g_ref[...], NORM_EPS)
    proj = _bdot(xn, w_ref[...])
    d = x_ref.shape[1]
    if make_tables:
        pos_ref, freq_ref, rot_ref, lin_ref, cos_ref, lo_ref, hi_ref = refs
        ang = pos_ref[...] * freq_ref[0:1, :]
        cos = jnp.cos(ang)
        sin = jnp.sin(ang)
        sin_lo = sin * freq_ref[1:2, :]
        sin_hi = sin * freq_ref[2:3, :]
        cos_ref[...] = cos
        lo_ref[...] = sin_lo
        hi_ref[...] = sin_hi
    else:
        cos_ref, lo_ref, hi_ref, rot_ref, lin_ref = refs
        cos, sin_lo, sin_hi = cos_ref[...], lo_ref[...], hi_ref[...]
    if scale != 1.0:
        cos, sin_lo, sin_hi = cos * scale, sin_lo * scale, sin_hi * scale
    half = ROT_DIM // 2
    for c in range(d // LANES):
        cols = slice(c * LANES, (c + 1) * LANES)
        t = proj[:, cols]
        rot = (t * cos + pltpu.roll(t, LANES - half, axis=1) * sin_lo
               + pltpu.roll(t, half, axis=1) * sin_hi)
        if rot_t:
            rot_ref[cols, :] = rot.T.astype(rot_ref.dtype)
        else:
            rot_ref[:, cols] = rot.astype(rot_ref.dtype)
        if lin_t:
            lin_ref[cols, :] = proj[:, d + c * LANES:d + (c + 1) * LANES].T.astype(lin_ref.dtype)
    if not lin_t:
        lin_ref[...] = proj[:, d:].astype(lin_ref.dtype)


def _proj_rope(x, g, w, rope, scale, rot_t, lin_t, lin_dtype):
    t, d = x.shape
    make_tables = len(rope) == 2
    tok = pl.BlockSpec((TM_WIDE, d), lambda i: (i, 0))
    tok_t = pl.BlockSpec((d, TM_WIDE), lambda i: (0, i))
    table = pl.BlockSpec((TM_WIDE, LANES), lambda i: (i, 0))

    def full(a):
        return pl.BlockSpec(a.shape, lambda i: (0,) * a.ndim)

    def shape(transposed, dtype):
        return jax.ShapeDtypeStruct((d, t) if transposed else (t, d), dtype)

    if make_tables:
        rope_specs = [table, full(rope[1])]
    else:
        rope_specs = [table] * 3
    out_specs = [tok_t if rot_t else tok, tok_t if lin_t else tok]
    out_shape = [shape(rot_t, BF16), shape(lin_t, lin_dtype)]
    if make_tables:
        out_specs += [table] * 3
        out_shape += [jax.ShapeDtypeStruct((t, LANES), F32)] * 3
    return pl.pallas_call(
        functools.partial(_proj_rope_kernel, scale, rot_t, lin_t, make_tables),
        grid=(t // TM_WIDE,),
        in_specs=[tok, full(g), full(w)] + rope_specs,
        out_specs=out_specs,
        out_shape=out_shape,
        compiler_params=_params(("arbitrary",)),
        name="proj_rope",
    )(x, g, w, *rope)


def _diff_attn_kernel(lam_init, qt_ref, qn_ref, k_ref, vt_ref, lam_ref, sub_ref, bias_ref, out_ref,
                      s_scr, first_max, acc_scr):
    i = pl.program_id(2)

    def stacked(qt):
        row_map = lax.broadcasted_iota(jnp.int32, qt.shape, 0) // B_QK
        zero = jnp.zeros_like(qt)
        return jnp.concatenate([jnp.where(row_map == 0, qt, zero), jnp.where(row_map == 1, qt, zero)],
                               axis=1)

    qs = stacked(qt_ref[...])
    lam = (jnp.exp(jnp.sum(lam_ref[0:1, :] * lam_ref[1:2, :], axis=1, keepdims=True))
           - jnp.exp(jnp.sum(lam_ref[2:3, :] * lam_ref[3:4, :], axis=1, keepdims=True)) + lam_init)

    def scores(slot, j, q=qs, diagonal=False):
        kb = k_ref[pl.ds(pl.multiple_of(j * BK, BK), BK), :]
        s = jnp.dot(kb, q, preferred_element_type=F32)
        if diagonal:
            s = s + bias_ref[...]
        s_scr[slot] = s
        return jnp.max(s, axis=0, keepdims=True)

    def next_first():
        first_max[...] = scores(2, 0, stacked(qn_ref[...]))

    def update(slot, j, s_max, m_prev, l_prev, diagonal=False):
        s = s_scr[slot]
        if diagonal:
            s = s + bias_ref[...]
            s_max = jnp.max(s, axis=0, keepdims=True)
        m_new = jnp.maximum(m_prev, s_max)
        p = jnp.exp2(s - m_new)
        alpha = jnp.exp2(m_prev - m_new)
        vb = jnp.concatenate([vt_ref[:, pl.ds(pl.multiple_of(j * BK, BK), BK)],
                              jnp.ones((ONES_ROWS, BK), BF16)], axis=0)
        pv = jnp.dot(vb, p.astype(BF16), preferred_element_type=F32)
        l_new = alpha * l_prev + pv[LANES:LANES + 1, :]
        acc_scr[...] = alpha * acc_scr[...] + pv[:LANES, :]
        return m_new, l_new

    acc_scr[...] = jnp.zeros_like(acc_scr)
    m0 = jnp.full((1, 2 * BQ), -jnp.inf, F32)
    l0 = jnp.zeros((1, 2 * BQ), F32)

    @pl.when(i == 0)
    def _():
        first_max[...] = scores(2, 0)

    def pair(j, carry):
        max1, m, l = carry
        max0 = scores(0, j + 1)
        m, l = update(1, j, max1, m, l)
        max1 = scores(1, j + 2)
        m, l = update(0, j + 1, max0, m, l)
        return max1, m, l

    def first_tile(_):
        m, l = update(2, 0, None, m0, l0, diagonal=True)
        next_first()
        return m, l

    def general(_):
        max1 = scores(1, 1)
        m, l = update(2, 0, first_max[...], m0, l0)
        n_full = i - 1
        carry = lax.fori_loop(
            0, n_full // 8,
            lambda t, c: pair(8 * t + 7, pair(8 * t + 5, pair(8 * t + 3, pair(8 * t + 1, c)))),
            (max1, m, l))
        rest = n_full % 8
        base = 1 + n_full - rest
        carry = lax.fori_loop(0, rest // 4, lambda t, c: pair(base + 2, pair(base, c)), carry)
        carry = lax.fori_loop(0, (rest % 4) // 2, lambda t, c: pair(base + rest - rest % 4, c), carry)

        def odd_tail(carry):
            max1, m, l = carry
            max0 = scores(0, i, diagonal=True)
            m, l = update(1, i - 1, max1, m, l)
            next_first()
            return update(0, i, max0, m, l)

        def even_tail(carry):
            _, m, l = carry
            next_first()
            return update(1, i, None, m, l, diagonal=True)

        return lax.cond(n_full % 2 == 1, odd_tail, even_tail, carry)

    m, l = lax.cond(i == 0, first_tile, general, 0)

    o_all = acc_scr[...] / l
    o_t = o_all[:, :BQ] - lam * o_all[:, BQ:]
    o_t = o_t * lax.rsqrt(jnp.mean(o_t * o_t, axis=0, keepdims=True) + SUBLN_EPS)
    out_ref[...] = (o_t.T * (sub_ref[...] * (1.0 - lam_init))).astype(out_ref.dtype)


def _diff_attn(qt, k, vt, lam_vecs, subln, lam_init, bsz, seq):
    d, t = qt.shape
    nq = seq // BQ
    heads = d // LANES

    def full(a):
        return pl.BlockSpec(a.shape, lambda b, h, i: (0,) * a.ndim, pipeline_mode=pl.Buffered(1))

    kpos = jnp.arange(BK)[:, None]
    qpos = jnp.arange(2 * BQ)[None, :] % BQ
    bias = jnp.where(kpos <= qpos, 0.0, -jnp.inf).astype(F32)

    return pl.pallas_call(
        functools.partial(_diff_attn_kernel, lam_init),
        grid=(bsz, heads, nq),
        in_specs=[pl.BlockSpec((LANES, BQ), lambda b, h, i: (h, b * nq + i)),
                  pl.BlockSpec((LANES, BQ), lambda b, h, i: (h, b * nq + jnp.minimum(i + 1, nq - 1))),
                  pl.BlockSpec((seq, LANES), lambda b, h, i: (b, h)),
                  pl.BlockSpec((LANES, seq), lambda b, h, i: (h, b)),
                  full(lam_vecs), full(subln), full(bias)],
        out_specs=pl.BlockSpec((BQ, LANES), lambda b, h, i: (b * nq + i, h)),
        out_shape=jax.ShapeDtypeStruct((t, d), BF16),
        scratch_shapes=[pltpu.VMEM((3, BK, 2 * BQ), F32), pltpu.VMEM((1, 2 * BQ), F32),
                        pltpu.VMEM((LANES, 2 * BQ), F32)],
        compiler_params=_params(("arbitrary", "arbitrary", "arbitrary")),
        name="diff_attn",
    )(qt, qt, k, vt, lam_vecs, subln, bias)


def _attn_post_kernel(final, o_ref, gate_ref, x_ref, wout_ref, fin_ref, out_ref):
    gate = gate_ref[...].astype(F32)
    o = o_ref[...].astype(F32) * (gate * _sigmoid(gate))
    out = x_ref[...] + _bdot(o, wout_ref[...])
    if final:
        out = _rms(out, fin_ref[...], NORM_EPS)
    out_ref[...] = out


def _attn_post(o, gate, x, w_out, fin, final):
    t, d = x.shape
    tok = pl.BlockSpec((TM_WIDE, d), lambda i: (i, 0))

    def full(a):
        return pl.BlockSpec(a.shape, lambda i: (0,) * a.ndim)

    return pl.pallas_call(
        functools.partial(_attn_post_kernel, final),
        grid=(t // TM_WIDE,),
        in_specs=[tok, tok, tok, full(w_out), full(fin)],
        out_specs=tok,
        out_shape=jax.ShapeDtypeStruct((t, d), F32),
        compiler_params=_params(("arbitrary",)),
        name="attn_post",
    )(o, gate, x, w_out, fin)


def _rope_tables():
    lane = jnp.arange(LANES) % B_QK
    half = ROT_DIM // 2
    inv_freq = ROPE_THETA ** (-jnp.arange(0, ROT_DIM, 2, dtype=F32) / ROT_DIM)
    freq = jnp.where(lane < ROT_DIM, inv_freq[lane % half], 0.0)
    lo = jnp.where(lane < half, -1.0, 0.0)
    hi = jnp.where((lane >= half) & (lane < ROT_DIM), 1.0, 0.0)
    return jnp.zeros((8, LANES), F32).at[0].set(freq).at[1].set(lo).at[2].set(hi)


def kernel(x, positions, a_norm, a_mu, a_w_in, a_w0, a_w1, a_w2, a_a0, a_a1, a_a2, a_v0, a_v1, a_v2, a_k_k, a_k_a, a_r_k, a_ln_w, a_ln_b, a_w_out, kv_norm, w_kv, b_norm, b_w_in, b_lq1, b_lk1, b_lq2, b_lk2, b_subln, b_w_out, final_norm):
    bsz, seq, d = x.shape
    t = bsz * seq
    seg = (jnp.arange(d)[:, None] // A_HEAD == jnp.arange(LANES)[None, :]).astype(BF16)
    seg_t = seg.T

    v_first = None
    for layer in range(N_A):
        zero = jnp.zeros((d,), F32)
        v0 = a_v0[layer - 1] if layer > 0 else zero
        vecs = jnp.stack([*a_mu[layer], a_norm[layer], a_w0[layer], a_a0[layer], v0, a_k_k[layer],
                          a_k_a[layer], a_r_k[layer].reshape(d), a_ln_w[layer], a_ln_b[layer], zero])
        vres = None
        if layer > 0:
            vres = (a_v1[layer - 1].astype(BF16), a_v2[layer - 1].astype(BF16), v_first)
        r, lw, k, v, kk, b, g = _rwkv_pre(
            x, vecs, a_w_in[layer].astype(BF16), a_w1[layer].astype(BF16), a_w2[layer].astype(BF16),
            a_a1[layer].astype(BF16), a_a2[layer].astype(BF16), seg, seg_t, vres)
        if layer == 0:
            v_first = v
        y = _wkv(r, lw, k, v, kk, b)
        flat = lambda a: a.reshape(t, d)
        x = _rwkv_post(flat(y), flat(r), flat(k), flat(v), flat(g), flat(x), vecs,
                       a_w_out[layer].astype(BF16), seg, seg_t).reshape(bsz, seq, d)

    x = x.reshape(t, d)
    pos = jnp.broadcast_to(positions.reshape(t, 1).astype(F32), (t, LANES))
    freq = _rope_tables()
    k_sh, vt_sh, *tables = _proj_rope(x, kv_norm.reshape(1, d), w_kv.astype(BF16), (pos, freq), 1.0,
                                      False, True, BF16)
    n_b = b_norm.shape[0]
    for j in range(n_b):
        layer = N_A + j
        lam_init = 0.8 - 0.6 * math.exp(-0.3 * layer)
        qt, gate = _proj_rope(x, b_norm[j].reshape(1, d), b_w_in[j].astype(BF16), tables,
                              B_QK ** -0.5 * math.log2(math.e), True, False, BF16)
        lam_vecs = jnp.zeros((8, LANES), F32).at[:4, :B_QK].set(
            jnp.stack([b_lq1[j], b_lk1[j], b_lq2[j], b_lk2[j]]))
        o = _diff_attn(qt, k_sh, vt_sh, lam_vecs, b_subln[j].reshape(1, B_V), lam_init, bsz, seq)
        x = _attn_post(o, gate, x, b_w_out[j].astype(BF16), final_norm.reshape(1, d), j == n_b - 1)
    return x.reshape(bsz, seq, d)
```

```python
import functools
import math

import jax
import jax.numpy as jnp
from jax import lax
from jax.experimental import pallas as pl
from jax.experimental.pallas import tpu as pltpu

F32 = jnp.float32
BF16 = jnp.bfloat16

A_HEAD = 64
B_QK = 64
B_V = 128
ROT_DIM = B_QK // 4
ROPE_THETA = 500000.0
NORM_EPS = 1e-6
SUBLN_EPS = 1e-5
GN_EPS = 64e-5
N_A = 2

LANES = 128
VMEM_LIMIT = 48 * 1024 * 1024

CHUNK = 64
WKV_CHUNKS = 4
TM = 256
TM_WIDE = 1024
BQ = 512
BK = 512
ONES_ROWS = 16

_MU, _NORM, _W0, _A0, _V0, _KK, _KA, _RK, _LNW, _LNB = 0, 6, 7, 8, 9, 10, 11, 12, 13, 14


def _params(sem):
    return pltpu.CompilerParams(dimension_semantics=sem, vmem_limit_bytes=VMEM_LIMIT)


def _bdot(a, b):
    return jnp.dot(a.astype(BF16), b.astype(BF16), preferred_element_type=F32)


def _nt(a, b):
    return lax.dot_general(a.astype(BF16), b.astype(BF16), (((1,), (1,)), ((), ())),
                           preferred_element_type=F32)


def _tn(a, b):
    return lax.dot_general(a.astype(BF16), b.astype(BF16), (((0,), (0,)), ((), ())),
                           preferred_element_type=F32)


def _dot_hi_lo(x, m):
    hi = x.astype(BF16)
    lo = (x - hi.astype(F32)).astype(BF16)
    return (jnp.dot(hi, m, preferred_element_type=F32) + jnp.dot(lo, m, preferred_element_type=F32))


def _rms(x, g, eps):
    return x * lax.rsqrt(jnp.mean(x * x, axis=-1, keepdims=True) + eps) * g


def _sigmoid(x):
    return 1.0 / (1.0 + jnp.exp(-x))


def _rwkv_pre_kernel(has_vres, *refs):
    if has_vres:
        (x_ref, vec_ref, win_ref, w1_ref, w2_ref, a1_ref, a2_ref, e_ref, et_ref,
         v1_ref, v2_ref, vf_ref,
         r_out, lw_out, k_out, v_out, kk_out, b_out, g_out, carry) = refs
    else:
        (x_ref, vec_ref, win_ref, w1_ref, w2_ref, a1_ref, a2_ref, e_ref, et_ref,
         r_out, lw_out, k_out, v_out, kk_out, b_out, g_out, carry) = refs

    def vec(i):
        return vec_ref[i:i + 1, :]

    @pl.when(pl.program_id(1) == 0)
    def _():
        carry[...] = jnp.zeros_like(carry)

    xn = _rms(x_ref[...], vec(_NORM), NORM_EPS)
    tm = xn.shape[0]
    rolled = pltpu.roll(xn, 1, axis=0)
    row = lax.broadcasted_iota(jnp.int32, (tm, 1), 0)
    prev = jnp.where(row == 0, carry[7:8, :], rolled)
    carry[...] = xn[tm - 8:, :]
    xx = prev - xn

    def mix(p):
        return xn + xx * vec(_MU + p)

    r = _bdot(mix(0), win_ref[0])
    k = _bdot(mix(1), win_ref[1])
    xm_v = mix(2)
    v = _bdot(xm_v, win_ref[2])
    g_out[...] = _bdot(mix(3), win_ref[3]).astype(g_out.dtype)

    wl = vec(_W0) + _bdot(jnp.tanh(_bdot(mix(4), w1_ref[...])), w2_ref[...])
    lw_out[...] = -math.exp(-0.5) * _sigmoid(wl)
    lr = _sigmoid(vec(_A0) + _bdot(_bdot(mix(5), a1_ref[...]), a2_ref[...]))
    if has_vres:
        gate_v = _sigmoid(vec(_V0) + _bdot(_bdot(xm_v, v1_ref[...]), v2_ref[...]))
        v = v + (vf_ref[...].astype(F32) - v) * gate_v

    kk = k * vec(_KK)
    ss = _bdot(kk * kk, e_ref[...])
    inv = 1.0 / jnp.maximum(jnp.sqrt(ss), 1e-12)
    kk = kk * _dot_hi_lo(inv, et_ref[...])
    r_out[...] = r.astype(r_out.dtype)
    v_out[...] = v.astype(v_out.dtype)
    kk_out[...] = kk.astype(kk_out.dtype)
    b_out[...] = (kk * lr).astype(b_out.dtype)
    k_out[...] = (k * (1.0 + (lr - 1.0) * vec(_KA))).astype(k_out.dtype)


def _rwkv_pre(x, vecs, w_in, w1, w2, a1, a2, seg, seg_t, vres):
    bsz, seq, d = x.shape
    tok = pl.BlockSpec((None, TM, d), lambda b, j: (b, j, 0))

    def full(a):
        return pl.BlockSpec(a.shape, lambda b, j: (0,) * a.ndim, pipeline_mode=pl.Buffered(1))

    args = [x, vecs, w_in, w1, w2, a1, a2, seg, seg_t]
    specs = [tok] + [full(a) for a in args[1:]]
    if vres is not None:
        v1, v2, v_first = vres
        args += [v1, v2, v_first]
        specs += [full(v1), full(v2), tok]
    dtypes = [BF16, F32, BF16, BF16, BF16, BF16, BF16]
    return pl.pallas_call(
        functools.partial(_rwkv_pre_kernel, vres is not None),
        grid=(bsz, seq // TM),
        in_specs=specs,
        out_specs=[tok] * 7,
        out_shape=[jax.ShapeDtypeStruct((bsz, seq, d), dt) for dt in dtypes],
        scratch_shapes=[pltpu.VMEM((8, d), F32)],
        compiler_params=_params(("arbitrary", "arbitrary")),
        name="rwkv_pre",
    )(*args)


def _stack(x, lane_head):
    x = x.astype(BF16)
    zero = jnp.zeros_like(x)
    return jnp.concatenate([jnp.where(lane_head == 0, x, zero),
                            jnp.where(lane_head == 1, x, zero)], axis=0)


def _wkv_prepare(ops, tri):
    units = range(len(ops))
    lane = lax.broadcasted_iota(jnp.int32, (CHUNK, LANES), 1)
    lane_head = lane // A_HEAD

    def cumsum(lw):
        h1 = lw.astype(BF16)
        r1 = lw - h1.astype(F32)
        h2 = r1.astype(BF16)
        h3 = (r1 - h2.astype(F32)).astype(BF16)
        return (jnp.dot(tri, h1, preferred_element_type=F32) + jnp.dot(tri, h2, preferred_element_type=F32)
                + jnp.dot(tri, h3, preferred_element_type=F32))

    cls = [cumsum(ops[u][1]) for u in units]

    def prep(op, cl):
        r, lw, k, v, kk, b = (t.astype(F32) for t in op)
        e_neg = jnp.exp(-cl)
        e_end = jnp.exp(cl[CHUNK - 1:CHUNK, :])
        bt = b * e_neg
        kt = k * e_neg
        a_n = (-kk * jnp.exp(cl - lw)).astype(BF16)
        r_n = (r * jnp.exp(cl)).astype(BF16)
        return dict(
            e_end=e_end, a_n=a_n, r_n=r_n, v_n=v.astype(BF16), v_s=_stack(v, lane_head),
            ar=jnp.concatenate([a_n, r_n], axis=0),
            bk=jnp.concatenate([_stack(bt, lane_head), _stack(kt, lane_head)], axis=0),
            bkh=jnp.concatenate([bt * e_end, kt * e_end], axis=0).astype(BF16))

    pre = [prep(ops[u], cls[u]) for u in units]
    sc = [_nt(pre[u]["ar"], pre[u]["bk"]) for u in units]

    row = lax.broadcasted_iota(jnp.int32, (CHUNK, LANES), 0)
    col = lane % A_HEAD
    strict = col < row
    incl = col <= row
    eye = (col == row).astype(F32)
    a_ab = [jnp.where(strict, sc[u][:CHUNK, :LANES], 0.0) for u in units]
    a_ak = [jnp.where(strict, sc[u][:CHUNK, LANES:], 0.0).astype(BF16) for u in units]
    a_r = [jnp.concatenate([jnp.where(incl, sc[u][CHUNK:, :LANES], 0.0),
                            jnp.where(incl, sc[u][CHUNK:, LANES:], 0.0)], axis=1).astype(BF16)
           for u in units]

    t_inv = [eye + a_ab[u] for u in units]
    pw = [_bdot(a_ab[u], _stack(a_ab[u], lane_head)) for u in units]
    for _ in range(int(math.log2(CHUNK)) - 2):
        both = [_bdot(jnp.concatenate([t_inv[u], pw[u]], axis=0), _stack(pw[u], lane_head))
                for u in units]
        t_inv = [t_inv[u] + both[u][:CHUNK] for u in units]
        pw = [both[u][CHUNK:] for u in units]
    t_inv = [t_inv[u] + _bdot(t_inv[u], _stack(pw[u], lane_head)) for u in units]

    akv = [_bdot(a_ak[u], pre[u]["v_s"]) for u in units]
    return [dict(pre[u], t_inv=t_inv[u].astype(BF16), akv=akv[u], a_r=a_r[u]) for u in units]


def _wkv_advance(prep, states):
    units = range(len(prep))
    lane_head = lax.broadcasted_iota(jnp.int32, (CHUNK, LANES), 1) // A_HEAD
    st = [states[u].astype(BF16) for u in units]
    ar_st = [_nt(prep[u]["ar"], st[u]) for u in units]
    rhs = [ar_st[u][:CHUNK] + prep[u]["akv"] for u in units]
    us = [jnp.dot(prep[u]["t_inv"], _stack(rhs[u], lane_head), preferred_element_type=F32)
          for u in units]
    uv_n = [jnp.concatenate([us[u].astype(BF16), prep[u]["v_n"]], axis=0) for u in units]
    same_head = (lax.broadcasted_iota(jnp.int32, (LANES, LANES), 0) // A_HEAD
                 == lax.broadcasted_iota(jnp.int32, (LANES, LANES), 1) // A_HEAD)
    new_states = [states[u] * prep[u]["e_end"] + jnp.where(same_head, _tn(uv_n[u], prep[u]["bkh"]), 0.0)
                  for u in units]
    ys = [ar_st[u][CHUNK:]
          + _bdot(prep[u]["a_r"], jnp.concatenate([_stack(us[u], lane_head), prep[u]["v_s"]], axis=0))
          for u in units]
    return ys, new_states


def _wkv_kernel(r_ref, lw_ref, k_ref, v_ref, kk_ref, b_ref, y_ref, state):
    @pl.when(pl.program_id(0) == 0)
    def _():
        state[...] = jnp.zeros_like(state)

    tri = (lax.broadcasted_iota(jnp.int32, (CHUNK, CHUNK), 1)
           <= lax.broadcasted_iota(jnp.int32, (CHUNK, CHUNK), 0)).astype(BF16)
    bsz, _, d = r_ref.shape
    pairs = [(bi, slice(p * LANES, (p + 1) * LANES)) for bi in range(bsz) for p in range(d // LANES)]
    rows = [slice(c * CHUNK, (c + 1) * CHUNK) for c in range(WKV_CHUNKS)]
    prep = _wkv_prepare([tuple(ref[bi, rs, sl] for ref in (r_ref, lw_ref, k_ref, v_ref, kk_ref, b_ref))
                         for rs in rows for bi, sl in pairs], tri)
    states = [state[u] for u in range(len(pairs))]
    for c, rs in enumerate(rows):
        ys, states = _wkv_advance(prep[c * len(pairs):(c + 1) * len(pairs)], states)
        for u, (bi, sl) in enumerate(pairs):
            y_ref[bi, rs, sl] = ys[u].astype(y_ref.dtype)
    for u in range(len(pairs)):
        state[u] = states[u]


def _wkv(r, lw, k, v, kk, b):
    bsz, seq, d = r.shape
    tok = pl.BlockSpec((bsz, WKV_CHUNKS * CHUNK, d), lambda j: (0, j, 0))
    return pl.pallas_call(
        _wkv_kernel,
        grid=(seq // (WKV_CHUNKS * CHUNK),),
        in_specs=[tok] * 6,
        out_specs=tok,
        out_shape=jax.ShapeDtypeStruct((bsz, seq, d), BF16),
        scratch_shapes=[pltpu.VMEM((bsz * d // LANES, LANES, LANES), F32)],
        compiler_params=_params(("arbitrary",)),
        name="wkv",
    )(r, lw, k, v, kk, b)


def _rwkv_post_kernel(y_ref, r_ref, k_ref, v_ref, g_ref, x_ref, vec_ref, wout_ref, e_ref, et_ref,
                      out_ref):
    def vec(i):
        return vec_ref[i:i + 1, :]

    seg = e_ref[...]
    seg_t = et_ref[...]

    def head_sum(t):
        return _bdot(t, seg)

    def spread(t):
        return _dot_hi_lo(t, seg_t)

    y = y_ref[...].astype(F32)
    yc = y - spread(head_sum(y) * (1.0 / A_HEAD))
    rstd = lax.rsqrt(head_sum(yc * yc) * (1.0 / A_HEAD) + GN_EPS)
    yn = yc * spread(rstd) * vec(_LNW) + vec(_LNB)
    rk = r_ref[...].astype(F32) * k_ref[...].astype(F32) * vec(_RK)
    bonus = spread(head_sum(rk)) * v_ref[...].astype(F32)
    g = g_ref[...].astype(F32)
    o = (yn + bonus) * (g * _sigmoid(g))
    out_ref[...] = x_ref[...] + _bdot(o, wout_ref[...])


def _rwkv_post(y, r, k, v, g, x, vecs, w_out, seg, seg_t):
    t, d = x.shape
    tok = pl.BlockSpec((TM_WIDE, d), lambda i: (i, 0))

    def full(a):
        return pl.BlockSpec(a.shape, lambda i: (0,) * a.ndim)

    return pl.pallas_call(
        _rwkv_post_kernel,
        grid=(t // TM_WIDE,),
        in_specs=[tok] * 6 + [full(vecs), full(w_out), full(seg), full(seg_t)],
        out_specs=tok,
        out_shape=jax.ShapeDtypeStruct((t, d), F32),
        compiler_params=_params(("arbitrary",)),
        name="rwkv_post",
    )(y, r, k, v, g, x, vecs, w_out, seg, seg_t)


def _proj_rope_kernel(scale, rot_t, lin_t, make_tables, x_ref, g_ref, w_ref, *refs):
    xn = _rms(x_ref[...], g_ref[...], NORM_EPS)
    proj = _bdot(xn, w_ref[...])
    d = x_ref.shape[1]
    if make_tables:
        pos_ref, freq_ref, rot_ref, lin_ref, cos_ref, lo_ref, hi_ref = refs
        ang = pos_ref[...] * freq_ref[0:1, :]
        cos = jnp.cos(ang)
        sin = jnp.sin(ang)
        sin_lo = sin * freq_ref[1:2, :]
        sin_hi = sin * freq_ref[2:3, :]
        cos_ref[...] = cos
        lo_ref[...] = sin_lo
        hi_ref[...] = sin_hi
    else:
        cos_ref, lo_ref, hi_ref, rot_ref, lin_ref = refs
        cos, sin_lo, sin_hi = cos_ref[...], lo_ref[...], hi_ref[...]
    if scale != 1.0:
        cos, sin_lo, sin_hi = cos * scale, sin_lo * scale, sin_hi * scale
    half = ROT_DIM // 2
    for c in range(d // LANES):
        cols = slice(c * LANES, (c + 1) * LANES)
        t = proj[:, cols]
        rot = (t * cos + pltpu.roll(t, LANES - half, axis=1) * sin_lo
               + pltpu.roll(t, half, axis=1) * sin_hi)
        if rot_t:
            rot_ref[cols, :] = rot.T.astype(rot_ref.dtype)
        else:
            rot_ref[:, cols] = rot.astype(rot_ref.dtype)
        if lin_t:
            lin_ref[cols, :] = proj[:, d + c * LANES:d + (c + 1) * LANES].T.astype(lin_ref.dtype)
    if not lin_t:
        lin_ref[...] = proj[:, d:].astype(lin_ref.dtype)


def _proj_rope(x, g, w, rope, scale, rot_t, lin_t, lin_dtype):
    t, d = x.shape
    make_tables = len(rope) == 2
    tok = pl.BlockSpec((TM_WIDE, d), lambda i: (i, 0))
    tok_t = pl.BlockSpec((d, TM_WIDE), lambda i: (0, i))
    table = pl.BlockSpec((TM_WIDE, LANES), lambda i: (i, 0))

    def full(a):
        return pl.BlockSpec(a.shape, lambda i: (0,) * a.ndim)

    def shape(transposed, dtype):
        return jax.ShapeDtypeStruct((d, t) if transposed else (t, d), dtype)

    if make_tables:
        rope_specs = [table, full(rope[1])]
    else:
        rope_specs = [table] * 3
    out_specs = [tok_t if rot_t else tok, tok_t if lin_t else tok]
    out_shape = [shape(rot_t, BF16), shape(lin_t, lin_dtype)]
    if make_tables:
        out_specs += [table] * 3
        out_shape += [jax.ShapeDtypeStruct((t, LANES), F32)] * 3
    return pl.pallas_call(
        functools.partial(_proj_rope_kernel, scale, rot_t, lin_t, make_tables),
        grid=(t // TM_WIDE,),
        in_specs=[tok, full(g), full(w)] + rope_specs,
        out_specs=out_specs,
        out_shape=out_shape,
        compiler_params=_params(("arbitrary",)),
        name="proj_rope",
    )(x, g, w, *rope)


def _diff_attn_kernel(lam_init, qt_ref, qn_ref, k_ref, vt_ref, lam_ref, sub_ref, bias_ref, out_ref,
                      s_scr, first_max, acc_scr):
    i = pl.program_id(2)

    def stacked(qt):
        row_map = lax.broadcasted_iota(jnp.int32, qt.shape, 0) // B_QK
        zero = jnp.zeros_like(qt)
        return jnp.concatenate([jnp.where(row_map == 0, qt, zero), jnp.where(row_map == 1, qt, zero)],
                               axis=1)

    qs = stacked(qt_ref[...])
    lam = (jnp.exp(jnp.sum(lam_ref[0:1, :] * lam_ref[1:2, :], axis=1, keepdims=True))
           - jnp.exp(jnp.sum(lam_ref[2:3, :] * lam_ref[3:4, :], axis=1, keepdims=True)) + lam_init)

    def scores(slot, j, q=qs, diagonal=False):
        kb = k_ref[pl.ds(pl.multiple_of(j * BK, BK), BK), :]
        s = jnp.dot(kb, q, preferred_element_type=F32)
        if diagonal:
            s = s + bias_ref[...]
        s_scr[slot] = s
        return jnp.max(s, axis=0, keepdims=True)

    def next_first():
        first_max[...] = scores(2, 0, stacked(qn_ref[...]))

    def update(slot, j, s_max, m_prev, l_prev, diagonal=False):
        s = s_scr[slot]
        if diagonal:
            s = s + bias_ref[...]
            s_max = jnp.max(s, axis=0, keepdims=True)
        m_new = jnp.maximum(m_prev, s_max)
        p = jnp.exp2(s - m_new)
        alpha = jnp.exp2(m_prev - m_new)
        vb = jnp.concatenate([vt_ref[:, pl.ds(pl.multiple_of(j * BK, BK), BK)],
                              jnp.ones((ONES_ROWS, BK), BF16)], axis=0)
        pv = jnp.dot(vb, p.astype(BF16), preferred_element_type=F32)
        l_new = alpha * l_prev + pv[LANES:LANES + 1, :]
        acc_scr[...] = alpha * acc_scr[...] + pv[:LANES, :]
        return m_new, l_new

    acc_scr[...] = jnp.zeros_like(acc_scr)
    m0 = jnp.full((1, 2 * BQ), -jnp.inf, F32)
    l0 = jnp.zeros((1, 2 * BQ), F32)

    @pl.when(i == 0)
    def _():
        first_max[...] = scores(2, 0)

    def pair(j, carry):
        max1, m, l = carry
        max0 = scores(0, j + 1)
        m, l = update(1, j, max1, m, l)
        max1 = scores(1, j + 2)
        m, l = update(0, j + 1, max0, m, l)
        return max1, m, l

    def first_tile(_):
        m, l = update(2, 0, None, m0, l0, diagonal=True)
        next_first()
        return m, l

    def general(_):
        max1 = scores(1, 1)
        m, l = update(2, 0, first_max[...], m0, l0)
        n_full = i - 1
        carry = lax.fori_loop(
            0, n_full // 8,
            lambda t, c: pair(8 * t + 7, pair(8 * t + 5, pair(8 * t + 3, pair(8 * t + 1, c)))),
            (max1, m, l))
        rest = n_full % 8
        base = 1 + n_full - rest
        carry = lax.fori_loop(0, rest // 4, lambda t, c: pair(base + 2, pair(base, c)), carry)
        carry = lax.fori_loop(0, (rest % 4) // 2, lambda t, c: pair(base + rest - rest % 4, c), carry)

        def odd_tail(carry):
            max1, m, l = carry
            max0 = scores(0, i, diagonal=True)
            m, l = update(1, i - 1, max1, m, l)
            next_first()
            return update(0, i, max0, m, l)

        def even_tail(carry):
            _, m, l = carry
            next_first()
            return update(1, i, None, m, l, diagonal=True)

        return lax.cond(n_full % 2 == 1, odd_tail, even_tail, carry)

    m, l = lax.cond(i == 0, first_tile, general, 0)

    o_all = acc_scr[...] / l
    o_t = o_all[:, :BQ] - lam * o_all[:, BQ:]
    o_t = o_t * lax.rsqrt(jnp.mean(o_t * o_t, axis=0, keepdims=True) + SUBLN_EPS)
    out_ref[...] = (o_t.T * (sub_ref[...] * (1.0 - lam_init))).astype(out_ref.dtype)


def _diff_attn(qt, k, vt, lam_vecs, subln, lam_init, bsz, seq):
    d, t = qt.shape
    nq = seq // BQ
    heads = d // LANES

    def full(a):
        return pl.BlockSpec(a.shape, lambda b, h, i: (0,) * a.ndim, pipeline_mode=pl.Buffered(1))

    kpos = jnp.arange(BK)[:, None]
    qpos = jnp.arange(2 * BQ)[None, :] % BQ
    bias = jnp.where(kpos <= qpos, 0.0, -jnp.inf).astype(F32)

    return pl.pallas_call(
        functools.partial(_diff_attn_kernel, lam_init),
        grid=(bsz, heads, nq),
        in_specs=[pl.BlockSpec((LANES, BQ), lambda b, h, i: (h, b * nq + i)),
                  pl.BlockSpec((LANES, BQ), lambda b, h, i: (h, b * nq + jnp.minimum(i + 1, nq - 1))),
                  pl.BlockSpec((seq, LANES), lambda b, h, i: (b, h)),
                  pl.BlockSpec((LANES, seq), lambda b, h, i: (h, b)),
                  full(lam_vecs), full(subln), full(bias)],
        out_specs=pl.BlockSpec((BQ, LANES), lambda b, h, i: (b * nq + i, h)),
        out_shape=jax.ShapeDtypeStruct((t, d), BF16),
        scratch_shapes=[pltpu.VMEM((3, BK, 2 * BQ), F32), pltpu.VMEM((1, 2 * BQ), F32),
                        pltpu.VMEM((LANES, 2 * BQ), F32)],
        compiler_params=_params(("arbitrary", "arbitrary", "arbitrary")),
        name="diff_attn",
    )(qt, qt, k, vt, lam_vecs, subln, bias)


def _attn_post_kernel(final, o_ref, gate_ref, x_ref, wout_ref, fin_ref, out_ref):
    gate = gate_ref[...].astype(F32)
    o = o_ref[...].astype(F32) * (gate * _sigmoid(gate))
    out = x_ref[...] + _bdot(o, wout_ref[...])
    if final:
        out = _rms(out, fin_ref[...], NORM_EPS)
    out_ref[...] = out


def _attn_post(o, gate, x, w_out, fin, final):
    t, d = x.shape
    tok = pl.BlockSpec((TM_WIDE, d), lambda i: (i, 0))

    def full(a):
        return pl.BlockSpec(a.shape, lambda i: (0,) * a.ndim)

    return pl.pallas_call(
        functools.partial(_attn_post_kernel, final),
        grid=(t // TM_WIDE,),
        in_specs=[tok, tok, tok, full(w_out), full(fin)],
        out_specs=tok,
        out_shape=jax.ShapeDtypeStruct((t, d), F32),
        compiler_params=_params(("arbitrary",)),
        name="attn_post",
    )(o, gate, x, w_out, fin)


def _rope_tables():
    lane = jnp.arange(LANES) % B_QK
    half = ROT_DIM // 2
    inv_freq = ROPE_THETA ** (-jnp.arange(0, ROT_DIM, 2, dtype=F32) / ROT_DIM)
    freq = jnp.where(lane < ROT_DIM, inv_freq[lane % half], 0.0)
    lo = jnp.where(lane < half, -1.0, 0.0)
    hi = jnp.where((lane >= half) & (lane < ROT_DIM), 1.0, 0.0)
    return jnp.zeros((8, LANES), F32).at[0].set(freq).at[1].set(lo).at[2].set(hi)


def kernel(x, positions, a_norm, a_mu, a_w_in, a_w0, a_w1, a_w2, a_a0, a_a1, a_a2, a_v0, a_v1, a_v2, a_k_k, a_k_a, a_r_k, a_ln_w, a_ln_b, a_w_out, kv_norm, w_kv, b_norm, b_w_in, b_lq1, b_lk1, b_lq2, b_lk2, b_subln, b_w_out, final_norm):
    bsz, seq, d = x.shape
    t = bsz * seq
    seg = (jnp.arange(d)[:, None] // A_HEAD == jnp.arange(LANES)[None, :]).astype(BF16)
    seg_t = seg.T

    v_first = None
    for layer in range(N_A):
        zero = jnp.zeros((d,), F32)
        v0 = a_v0[layer - 1] if layer > 0 else zero
        vecs = jnp.stack([*a_mu[layer], a_norm[layer], a_w0[layer], a_a0[layer], v0, a_k_k[layer],
                          a_k_a[layer], a_r_k[layer].reshape(d), a_ln_w[layer], a_ln_b[layer], zero])
        vres = None
        if layer > 0:
            vres = (a_v1[layer - 1].astype(BF16), a_v2[layer - 1].astype(BF16), v_first)
        r, lw, k, v, kk, b, g = _rwkv_pre(
            x, vecs, a_w_in[layer].astype(BF16), a_w1[layer].astype(BF16), a_w2[layer].astype(BF16),
            a_a1[layer].astype(BF16), a_a2[layer].astype(BF16), seg, seg_t, vres)
        if layer == 0:
            v_first = v
        y = _wkv(r, lw, k, v, kk, b)
        flat = lambda a: a.reshape(t, d)
        x = _rwkv_post(flat(y), flat(r), flat(k), flat(v), flat(g), flat(x), vecs,
                       a_w_out[layer].astype(BF16), seg, seg_t).reshape(bsz, seq, d)

    x = x.reshape(t, d)
    pos = jnp.broadcast_to(positions.reshape(t, 1).astype(F32), (t, LANES))
    freq = _rope_tables()
    k_sh, vt_sh, *tables = _proj_rope(x, kv_norm.reshape(1, d), w_kv.astype(BF16), (pos, freq), 1.0,
                                      False, True, BF16)
    n_b = b_norm.shape[0]
    for j in range(n_b):
        layer = N_A + j
        lam_init = 0.8 - 0.6 * math.exp(-0.3 * layer)
        qt, gate = _proj_rope(x, b_norm[j].reshape(1, d), b_w_in[j].astype(BF16), tables,
                              B_QK ** -0.5 * math.log2(math.e), True, False, BF16)
        lam_vecs = jnp.zeros((8, LANES), F32).at[:4, :B_QK].set(
            jnp.stack([b_lq1[j], b_lk1[j], b_lq2[j], b_lk2[j]]))
        o = _diff_attn(qt, k_sh, vt_sh, lam_vecs, b_subln[j].reshape(1, B_V), lam_init, bsz, seq)
        x = _attn_post(o, gate, x, b_w_out[j].astype(BF16), final_norm.reshape(1, d), j == n_b - 1)
    return x.reshape(bsz, seq, d)
```

```python
import functools
import math

import jax
import jax.numpy as jnp
from jax import lax
from jax.experimental import pallas as pl
from jax.experimental.pallas import tpu as pltpu

F32 = jnp.float32
BF16 = jnp.bfloat16

A_HEAD = 64
B_QK = 64
B_V = 128
ROT_DIM = B_QK // 4
ROPE_THETA = 500000.0
NORM_EPS = 1e-6
SUBLN_EPS = 1e-5
GN_EPS = 64e-5
N_A = 2

LANES = 128
VMEM_LIMIT = 48 * 1024 * 1024

CHUNK = 64
WKV_CHUNKS = 4
TM = 256
TM_WIDE = 1024
BQ = 512
BK = 512
ONES_ROWS = 16
X_SLOTS = 3

_MU, _NORM, _W0, _A0, _V0, _KK, _KA, _RK, _LNW, _LNB = 0, 6, 7, 8, 9, 10, 11, 12, 13, 14


def _params(sem):
    return pltpu.CompilerParams(dimension_semantics=sem, vmem_limit_bytes=VMEM_LIMIT)


def _bdot(a, b):
    return jnp.dot(a.astype(BF16), b.astype(BF16), preferred_element_type=F32)


def _nt(a, b):
    return lax.dot_general(a.astype(BF16), b.astype(BF16), (((1,), (1,)), ((), ())),
                           preferred_element_type=F32)


def _tn(a, b):
    return lax.dot_general(a.astype(BF16), b.astype(BF16), (((0,), (0,)), ((), ())),
                           preferred_element_type=F32)


def _dot_hi_lo(x, m):
    hi = x.astype(BF16)
    lo = (x - hi.astype(F32)).astype(BF16)
    return (jnp.dot(hi, m, preferred_element_type=F32) + jnp.dot(lo, m, preferred_element_type=F32))


def _rms(x, g, eps):
    return x * lax.rsqrt(jnp.mean(x * x, axis=-1, keepdims=True) + eps) * g


def _sigmoid(x):
    return 1.0 / (1.0 + jnp.exp(-x))


def _rwkv_pre_kernel(has_vres, *refs):
    if has_vres:
        (x_ref, vec_ref, win_ref, w1_ref, w2_ref, a1_ref, a2_ref, e_ref, et_ref,
         v1_ref, v2_ref, vf_ref,
         r_out, lw_out, k_out, v_out, kk_out, b_out, g_out, carry) = refs
    else:
        (x_ref, vec_ref, win_ref, w1_ref, w2_ref, a1_ref, a2_ref, e_ref, et_ref,
         r_out, lw_out, k_out, v_out, kk_out, b_out, g_out, carry) = refs

    def vec(i):
        return vec_ref[i:i + 1, :]

    @pl.when(pl.program_id(1) == 0)
    def _():
        carry[...] = jnp.zeros_like(carry)

    xn = _rms(x_ref[...], vec(_NORM), NORM_EPS)
    tm = xn.shape[0]
    rolled = pltpu.roll(xn, 1, axis=0)
    row = lax.broadcasted_iota(jnp.int32, (tm, 1), 0)
    prev = jnp.where(row == 0, carry[7:8, :], rolled)
    carry[...] = xn[tm - 8:, :]
    xx = prev - xn

    def mix(p):
        return xn + xx * vec(_MU + p)

    r = _bdot(mix(0), win_ref[0])
    k = _bdot(mix(1), win_ref[1])
    xm_v = mix(2)
    v = _bdot(xm_v, win_ref[2])
    g_out[...] = _bdot(mix(3), win_ref[3]).astype(g_out.dtype)

    wl = vec(_W0) + _bdot(jnp.tanh(_bdot(mix(4), w1_ref[...])), w2_ref[...])
    lw_out[...] = -math.exp(-0.5) * _sigmoid(wl)
    lr = _sigmoid(vec(_A0) + _bdot(_bdot(mix(5), a1_ref[...]), a2_ref[...]))
    if has_vres:
        gate_v = _sigmoid(vec(_V0) + _bdot(_bdot(xm_v, v1_ref[...]), v2_ref[...]))
        v = v + (vf_ref[...].astype(F32) - v) * gate_v

    kk = k * vec(_KK)
    ss = _bdot(kk * kk, e_ref[...])
    inv = 1.0 / jnp.maximum(jnp.sqrt(ss), 1e-12)
    kk = kk * _dot_hi_lo(inv, et_ref[...])
    r_out[...] = r.astype(r_out.dtype)
    v_out[...] = v.astype(v_out.dtype)
    kk_out[...] = kk.astype(kk_out.dtype)
    b_out[...] = (kk * lr).astype(b_out.dtype)
    k_out[...] = (k * (1.0 + (lr - 1.0) * vec(_KA))).astype(k_out.dtype)


def _rwkv_pre(x, vecs, w_in, w1, w2, a1, a2, seg, seg_t, vres):
    bsz, seq, d = x.shape
    tok = pl.BlockSpec((None, TM, d), lambda b, j: (b, j, 0))

    def full(a):
        return pl.BlockSpec(a.shape, lambda b, j: (0,) * a.ndim, pipeline_mode=pl.Buffered(1))

    args = [x, vecs, w_in, w1, w2, a1, a2, seg, seg_t]
    specs = [tok] + [full(a) for a in args[1:]]
    if vres is not None:
        v1, v2, v_first = vres
        args += [v1, v2, v_first]
        specs += [full(v1), full(v2), tok]
    dtypes = [BF16, F32, BF16, BF16, BF16, BF16, BF16]
    return pl.pallas_call(
        functools.partial(_rwkv_pre_kernel, vres is not None),
        grid=(bsz, seq // TM),
        in_specs=specs,
        out_specs=[tok] * 7,
        out_shape=[jax.ShapeDtypeStruct((bsz, seq, d), dt) for dt in dtypes],
        scratch_shapes=[pltpu.VMEM((8, d), F32)],
        compiler_params=_params(("arbitrary", "arbitrary")),
        name="rwkv_pre",
    )(*args)


def _stack(x, lane_head):
    x = x.astype(BF16)
    zero = jnp.zeros_like(x)
    return jnp.concatenate([jnp.where(lane_head == 0, x, zero),
                            jnp.where(lane_head == 1, x, zero)], axis=0)


def _wkv_prepare(ops, tri):
    units = range(len(ops))
    lane = lax.broadcasted_iota(jnp.int32, (CHUNK, LANES), 1)
    lane_head = lane // A_HEAD

    def cumsum(lw):
        h1 = lw.astype(BF16)
        r1 = lw - h1.astype(F32)
        h2 = r1.astype(BF16)
        h3 = (r1 - h2.astype(F32)).astype(BF16)
        return (jnp.dot(tri, h1, preferred_element_type=F32) + jnp.dot(tri, h2, preferred_element_type=F32)
                + jnp.dot(tri, h3, preferred_element_type=F32))

    cls = [cumsum(ops[u][1]) for u in units]

    def prep(op, cl):
        r, lw, k, v, kk, b = (t.astype(F32) for t in op)
        e_neg = jnp.exp(-cl)
        e_end = jnp.exp(cl[CHUNK - 1:CHUNK, :])
        bt = b * e_neg
        kt = k * e_neg
        a_n = (-kk * jnp.exp(cl - lw)).astype(BF16)
        r_n = (r * jnp.exp(cl)).astype(BF16)
        return dict(
            e_end=e_end, a_n=a_n, r_n=r_n, v_n=v.astype(BF16), v_s=_stack(v, lane_head),
            ar=jnp.concatenate([a_n, r_n], axis=0),
            bk=jnp.concatenate([_stack(bt, lane_head), _stack(kt, lane_head)], axis=0),
            bkh=jnp.concatenate([bt * e_end, kt * e_end], axis=0).astype(BF16))

    pre = [prep(ops[u], cls[u]) for u in units]
    sc = [_nt(pre[u]["ar"], pre[u]["bk"]) for u in units]

    row = lax.broadcasted_iota(jnp.int32, (CHUNK, LANES), 0)
    col = lane % A_HEAD
    strict = col < row
    incl = col <= row
    eye = (col == row).astype(F32)
    a_ab = [jnp.where(strict, sc[u][:CHUNK, :LANES], 0.0) for u in units]
    a_ak = [jnp.where(strict, sc[u][:CHUNK, LANES:], 0.0).astype(BF16) for u in units]
    a_r = [jnp.concatenate([jnp.where(incl, sc[u][CHUNK:, :LANES], 0.0),
                            jnp.where(incl, sc[u][CHUNK:, LANES:], 0.0)], axis=1).astype(BF16)
           for u in units]

    t_inv = [eye + a_ab[u] for u in units]
    pw = [_bdot(a_ab[u], _stack(a_ab[u], lane_head)) for u in units]
    for _ in range(int(math.log2(CHUNK)) - 2):
        both = [_bdot(jnp.concatenate([t_inv[u], pw[u]], axis=0), _stack(pw[u], lane_head))
                for u in units]
        t_inv = [t_inv[u] + both[u][:CHUNK] for u in units]
        pw = [both[u][CHUNK:] for u in units]
    t_inv = [t_inv[u] + _bdot(t_inv[u], _stack(pw[u], lane_head)) for u in units]

    akv = [_bdot(a_ak[u], pre[u]["v_s"]) for u in units]
    return [dict(pre[u], t_inv=t_inv[u].astype(BF16), akv=akv[u], a_r=a_r[u]) for u in units]


def _wkv_advance(prep, states):
    units = range(len(prep))
    lane_head = lax.broadcasted_iota(jnp.int32, (CHUNK, LANES), 1) // A_HEAD
    st = [states[u].astype(BF16) for u in units]
    ar_st = [_nt(prep[u]["ar"], st[u]) for u in units]
    rhs = [ar_st[u][:CHUNK] + prep[u]["akv"] for u in units]
    us = [jnp.dot(prep[u]["t_inv"], _stack(rhs[u], lane_head), preferred_element_type=F32)
          for u in units]
    uv_n = [jnp.concatenate([us[u].astype(BF16), prep[u]["v_n"]], axis=0) for u in units]
    same_head = (lax.broadcasted_iota(jnp.int32, (LANES, LANES), 0) // A_HEAD
                 == lax.broadcasted_iota(jnp.int32, (LANES, LANES), 1) // A_HEAD)
    new_states = [states[u] * prep[u]["e_end"] + jnp.where(same_head, _tn(uv_n[u], prep[u]["bkh"]), 0.0)
                  for u in units]
    ys = [ar_st[u][CHUNK:]
          + _bdot(prep[u]["a_r"], jnp.concatenate([_stack(us[u], lane_head), prep[u]["v_s"]], axis=0))
          for u in units]
    return ys, new_states


def _wkv_kernel(r_ref, lw_ref, k_ref, v_ref, kk_ref, b_ref, y_ref, state):
    @pl.when(pl.program_id(0) == 0)
    def _():
        state[...] = jnp.zeros_like(state)

    tri = (lax.broadcasted_iota(jnp.int32, (CHUNK, CHUNK), 1)
           <= lax.broadcasted_iota(jnp.int32, (CHUNK, CHUNK), 0)).astype(BF16)
    bsz, _, d = r_ref.shape
    pairs = [(bi, slice(p * LANES, (p + 1) * LANES)) for bi in range(bsz) for p in range(d // LANES)]
    rows = [slice(c * CHUNK, (c + 1) * CHUNK) for c in range(WKV_CHUNKS)]
    prep = _wkv_prepare([tuple(ref[bi, rs, sl] for ref in (r_ref, lw_ref, k_ref, v_ref, kk_ref, b_ref))
                         for rs in rows for bi, sl in pairs], tri)
    states = [state[u] for u in range(len(pairs))]
    for c, rs in enumerate(rows):
        ys, states = _wkv_advance(prep[c * len(pairs):(c + 1) * len(pairs)], states)
        for u, (bi, sl) in enumerate(pairs):
            y_ref[bi, rs, sl] = ys[u].astype(y_ref.dtype)
    for u in range(len(pairs)):
        state[u] = states[u]


def _wkv(r, lw, k, v, kk, b):
    bsz, seq, d = r.shape
    tok = pl.BlockSpec((bsz, WKV_CHUNKS * CHUNK, d), lambda j: (0, j, 0))
    return pl.pallas_call(
        _wkv_kernel,
        grid=(seq // (WKV_CHUNKS * CHUNK),),
        in_specs=[tok] * 6,
        out_specs=tok,
        out_shape=jax.ShapeDtypeStruct((bsz, seq, d), BF16),
        scratch_shapes=[pltpu.VMEM((bsz * d // LANES, LANES, LANES), F32)],
        compiler_params=_params(("arbitrary",)),
        name="wkv",
    )(r, lw, k, v, kk, b)


def _rwkv_post_kernel(y_ref, r_ref, k_ref, v_ref, g_ref, x_ref, vec_ref, wout_ref, e_ref, et_ref,
                      out_ref):
    def vec(i):
        return vec_ref[i:i + 1, :]

    seg = e_ref[...]
    seg_t = et_ref[...]

    def head_sum(t):
        return _bdot(t, seg)

    def spread(t):
        return _dot_hi_lo(t, seg_t)

    y = y_ref[...].astype(F32)
    yc = y - spread(head_sum(y) * (1.0 / A_HEAD))
    rstd = lax.rsqrt(head_sum(yc * yc) * (1.0 / A_HEAD) + GN_EPS)
    yn = yc * spread(rstd) * vec(_LNW) + vec(_LNB)
    rk = r_ref[...].astype(F32) * k_ref[...].astype(F32) * vec(_RK)
    bonus = spread(head_sum(rk)) * v_ref[...].astype(F32)
    g = g_ref[...].astype(F32)
    o = (yn + bonus) * (g * _sigmoid(g))
    out_ref[...] = x_ref[...] + _bdot(o, wout_ref[...])


def _rwkv_post(y, r, k, v, g, x, vecs, w_out, seg, seg_t):
    t, d = x.shape
    tok = pl.BlockSpec((TM_WIDE, d), lambda i: (i, 0))

    def full(a):
        return pl.BlockSpec(a.shape, lambda i: (0,) * a.ndim)

    return pl.pallas_call(
        _rwkv_post_kernel,
        grid=(t // TM_WIDE,),
        in_specs=[tok] * 6 + [full(vecs), full(w_out), full(seg), full(seg_t)],
        out_specs=tok,
        out_shape=jax.ShapeDtypeStruct((t, d), F32),
        compiler_params=_params(("arbitrary",)),
        name="rwkv_post",
    )(y, r, k, v, g, x, vecs, w_out, seg, seg_t)


def _proj_rope_kernel(scale, rot_t, lin_t, make_tables, x_ref, g_ref, w_ref, *refs):
    xn = _rms(x_ref[...], g_ref[...], NORM_EPS)
    proj = _bdot(xn, w_ref[...])
    d = x_ref.shape[1]
    if make_tables:
        pos_ref, freq_ref, rot_ref, lin_ref, cos_ref, lo_ref, hi_ref = refs
        ang = pos_ref[...] * freq_ref[0:1, :]
        cos = jnp.cos(ang)
        sin = jnp.sin(ang)
        sin_lo = sin * freq_ref[1:2, :]
        sin_hi = sin * freq_ref[2:3, :]
        cos_ref[...] = cos
        lo_ref[...] = sin_lo
        hi_ref[...] = sin_hi
    else:
        cos_ref, lo_ref, hi_ref, rot_ref, lin_ref = refs
        cos, sin_lo, sin_hi = cos_ref[...], lo_ref[...], hi_ref[...]
    if scale != 1.0:
        cos, sin_lo, sin_hi = cos * scale, sin_lo * scale, sin_hi * scale
    half = ROT_DIM // 2
    for c in range(d // LANES):
        cols = slice(c * LANES, (c + 1) * LANES)
        t = proj[:, cols]
        rot = (t * cos + pltpu.roll(t, LANES - half, axis=1) * sin_lo
               + pltpu.roll(t, half, axis=1) * sin_hi)
        if rot_t:
            rot_ref[cols, :] = rot.T.astype(rot_ref.dtype)
        else:
            rot_ref[:, cols] = rot.astype(rot_ref.dtype)
        if lin_t:
            lin_ref[cols, :] = proj[:, d + c * LANES:d + (c + 1) * LANES].T.astype(lin_ref.dtype)
    if not lin_t:
        lin_ref[...] = proj[:, d:].astype(lin_ref.dtype)


def _proj_rope(x, g, w, rope, scale, rot_t, lin_t, lin_dtype):
    t, d = x.shape
    make_tables = len(rope) == 2
    tok = pl.BlockSpec((TM_WIDE, d), lambda i: (i, 0))
    tok_t = pl.BlockSpec((d, TM_WIDE), lambda i: (0, i))
    table = pl.BlockSpec((TM_WIDE, LANES), lambda i: (i, 0))

    def full(a):
        return pl.BlockSpec(a.shape, lambda i: (0,) * a.ndim)

    def shape(transposed, dtype):
        return jax.ShapeDtypeStruct((d, t) if transposed else (t, d), dtype)

    if make_tables:
        rope_specs = [table, full(rope[1])]
    else:
        rope_specs = [table] * 3
    out_specs = [tok_t if rot_t else tok, tok_t if lin_t else tok]
    out_shape = [shape(rot_t, BF16), shape(lin_t, lin_dtype)]
    if make_tables:
        out_specs += [table] * 3
        out_shape += [jax.ShapeDtypeStruct((t, LANES), F32)] * 3
    return pl.pallas_call(
        functools.partial(_proj_rope_kernel, scale, rot_t, lin_t, make_tables),
        grid=(t // TM_WIDE,),
        in_specs=[tok, full(g), full(w)] + rope_specs,
        out_specs=out_specs,
        out_shape=out_shape,
        compiler_params=_params(("arbitrary",)),
        name="proj_rope",
    )(x, g, w, *rope)


def _diff_attn_kernel(lam_init, qt_ref, qn_ref, k_ref, vt_ref, lam_ref, sub_ref, bias_ref, out_ref,
                      s_scr, first_max, acc_scr):
    i = pl.program_id(2)

    def stacked(qt):
        row_map = lax.broadcasted_iota(jnp.int32, qt.shape, 0) // B_QK
        zero = jnp.zeros_like(qt)
        return jnp.concatenate([jnp.where(row_map == 0, qt, zero), jnp.where(row_map == 1, qt, zero)],
                               axis=1)

    qs = stacked(qt_ref[...])
    lam = (jnp.exp(jnp.sum(lam_ref[0:1, :] * lam_ref[1:2, :], axis=1, keepdims=True))
           - jnp.exp(jnp.sum(lam_ref[2:3, :] * lam_ref[3:4, :], axis=1, keepdims=True)) + lam_init)

    def scores(slot, j, q=qs, diagonal=False):
        kb = k_ref[pl.ds(pl.multiple_of(j * BK, BK), BK), :]
        s = jnp.dot(kb, q, preferred_element_type=F32)
        if diagonal:
            s = s + bias_ref[...]
        s_scr[slot] = s
        return jnp.max(s, axis=0, keepdims=True)

    def next_first():
        first_max[...] = scores(2, 0, stacked(qn_ref[...]))

    def update(slot, j, s_max, m_prev, l_prev, diagonal=False):
        s = s_scr[slot]
        if diagonal:
            s = s + bias_ref[...]
            s_max = jnp.max(s, axis=0, keepdims=True)
        m_new = jnp.maximum(m_prev, s_max)
        p = jnp.exp2(s - m_new)
        alpha = jnp.exp2(m_prev - m_new)
        vb = jnp.concatenate([vt_ref[:, pl.ds(pl.multiple_of(j * BK, BK), BK)],
                              jnp.ones((ONES_ROWS, BK), BF16)], axis=0)
        pv = jnp.dot(vb, p.astype(BF16), preferred_element_type=F32)
        l_new = alpha * l_prev + pv[LANES:LANES + 1, :]
        acc_scr[...] = alpha * acc_scr[...] + pv[:LANES, :]
        return m_new, l_new

    acc_scr[...] = jnp.zeros_like(acc_scr)
    m0 = jnp.full((1, 2 * BQ), -jnp.inf, F32)
    l0 = jnp.zeros((1, 2 * BQ), F32)

    @pl.when(i == 0)
    def _():
        first_max[...] = scores(2, 0)

    def pair(j, carry):
        max1, m, l = carry
        max0 = scores(0, j + 1)
        m, l = update(1, j, max1, m, l)
        max1 = scores(1, j + 2)
        m, l = update(0, j + 1, max0, m, l)
        return max1, m, l

    def first_tile(_):
        m, l = update(2, 0, None, m0, l0, diagonal=True)
        next_first()
        return m, l

    def general(_):
        max1 = scores(1, 1)
        m, l = update(2, 0, first_max[...], m0, l0)
        n_full = i - 1
        carry = lax.fori_loop(
            0, n_full // 8,
            lambda t, c: pair(8 * t + 7, pair(8 * t + 5, pair(8 * t + 3, pair(8 * t + 1, c)))),
            (max1, m, l))
        rest = n_full % 8
        base = 1 + n_full - rest
        carry = lax.fori_loop(0, rest // 4, lambda t, c: pair(base + 2, pair(base, c)), carry)
        carry = lax.fori_loop(0, (rest % 4) // 2, lambda t, c: pair(base + rest - rest % 4, c), carry)

        def odd_tail(carry):
            max1, m, l = carry
            max0 = scores(0, i, diagonal=True)
            m, l = update(1, i - 1, max1, m, l)
            next_first()
            return update(0, i, max0, m, l)

        def even_tail(carry):
            _, m, l = carry
            next_first()
            return update(1, i, None, m, l, diagonal=True)

        return lax.cond(n_full % 2 == 1, odd_tail, even_tail, carry)

    m, l = lax.cond(i == 0, first_tile, general, 0)

    o_all = acc_scr[...] / l
    o_t = o_all[:, :BQ] - lam * o_all[:, BQ:]
    o_t = o_t * lax.rsqrt(jnp.mean(o_t * o_t, axis=0, keepdims=True) + SUBLN_EPS)
    out_ref[...] = (o_t.T * (sub_ref[...] * (1.0 - lam_init))).astype(out_ref.dtype)


def _diff_attn(qt, k, vt, lam_vecs, subln, lam_init, bsz, seq):
    d, t = qt.shape
    nq = seq // BQ
    heads = d // LANES

    def full(a):
        return pl.BlockSpec(a.shape, lambda b, h, i: (0,) * a.ndim, pipeline_mode=pl.Buffered(1))

    kpos = jnp.arange(BK)[:, None]
    qpos = jnp.arange(2 * BQ)[None, :] % BQ
    bias = jnp.where(kpos <= qpos, 0.0, -jnp.inf).astype(F32)

    return pl.pallas_call(
        functools.partial(_diff_attn_kernel, lam_init),
        grid=(bsz, heads, nq),
        in_specs=[pl.BlockSpec((LANES, BQ), lambda b, h, i: (h, b * nq + i)),
                  pl.BlockSpec((LANES, BQ), lambda b, h, i: (h, b * nq + jnp.minimum(i + 1, nq - 1))),
                  pl.BlockSpec((seq, LANES), lambda b, h, i: (b, h)),
                  pl.BlockSpec((LANES, seq), lambda b, h, i: (h, b)),
                  full(lam_vecs), full(subln), full(bias)],
        out_specs=pl.BlockSpec((BQ, LANES), lambda b, h, i: (b * nq + i, h)),
        out_shape=jax.ShapeDtypeStruct((t, d), BF16),
        scratch_shapes=[pltpu.VMEM((3, BK, 2 * BQ), F32), pltpu.VMEM((1, 2 * BQ), F32),
                        pltpu.VMEM((LANES, 2 * BQ), F32)],
        compiler_params=_params(("arbitrary", "arbitrary", "arbitrary")),
        name="diff_attn",
    )(qt, qt, k, vt, lam_vecs, subln, bias)


def _attn_post_kernel(final, steps, o_ref, gate_ref, x_hbm, wout_ref, fin_ref, out_ref, x_buf, x_sem):
    i = pl.program_id(0)

    def x_copy(step):
        slot = step % X_SLOTS
        rows = pl.ds(pl.multiple_of(step * TM_WIDE, TM_WIDE), TM_WIDE)
        return pltpu.make_async_copy(x_hbm.at[rows, :], x_buf.at[slot], x_sem.at[slot])

    @pl.when(i == 0)
    def _():
        for step in range(min(X_SLOTS - 1, steps)):
            x_copy(step).start()

    @pl.when(i + (X_SLOTS - 1) < steps)
    def _():
        x_copy(i + (X_SLOTS - 1)).start()

    gate = gate_ref[...].astype(F32)
    o = o_ref[...].astype(F32) * (gate * _sigmoid(gate))
    proj = _bdot(o, wout_ref[...])
    x_copy(i).wait()
    out = x_buf[i % X_SLOTS] + proj
    if final:
        out = _rms(out, fin_ref[...], NORM_EPS)
    out_ref[...] = out


def _attn_post(o, gate, x, w_out, fin, final):
    t, d = x.shape
    steps = t // TM_WIDE
    tok = pl.BlockSpec((TM_WIDE, d), lambda i: (i, 0))

    def full(a):
        return pl.BlockSpec(a.shape, lambda i: (0,) * a.ndim)

    return pl.pallas_call(
        functools.partial(_attn_post_kernel, final, steps),
        grid=(steps,),
        in_specs=[tok, tok, pl.BlockSpec(memory_space=pl.ANY), full(w_out), full(fin)],
        out_specs=tok,
        out_shape=jax.ShapeDtypeStruct((t, d), F32),
        scratch_shapes=[pltpu.VMEM((X_SLOTS, TM_WIDE, d), F32), pltpu.SemaphoreType.DMA((X_SLOTS,))],
        compiler_params=_params(("arbitrary",)),
        name="attn_post",
    )(o, gate, x, w_out, fin)


def _rope_tables():
    lane = jnp.arange(LANES) % B_QK
    half = ROT_DIM // 2
    inv_freq = ROPE_THETA ** (-jnp.arange(0, ROT_DIM, 2, dtype=F32) / ROT_DIM)
    freq = jnp.where(lane < ROT_DIM, inv_freq[lane % half], 0.0)
    lo = jnp.where(lane < half, -1.0, 0.0)
    hi = jnp.where((lane >= half) & (lane < ROT_DIM), 1.0, 0.0)
    return jnp.zeros((8, LANES), F32).at[0].set(freq).at[1].set(lo).at[2].set(hi)


def kernel(x, positions, a_norm, a_mu, a_w_in, a_w0, a_w1, a_w2, a_a0, a_a1, a_a2, a_v0, a_v1, a_v2, a_k_k, a_k_a, a_r_k, a_ln_w, a_ln_b, a_w_out, kv_norm, w_kv, b_norm, b_w_in, b_lq1, b_lk1, b_lq2, b_lk2, b_subln, b_w_out, final_norm):
    bsz, seq, d = x.shape
    t = bsz * seq
    seg = (jnp.arange(d)[:, None] // A_HEAD == jnp.arange(LANES)[None, :]).astype(BF16)
    seg_t = seg.T

    v_first = None
    for layer in range(N_A):
        zero = jnp.zeros((d,), F32)
        v0 = a_v0[layer - 1] if layer > 0 else zero
        vecs = jnp.stack([*a_mu[layer], a_norm[layer], a_w0[layer], a_a0[layer], v0, a_k_k[layer],
                          a_k_a[layer], a_r_k[layer].reshape(d), a_ln_w[layer], a_ln_b[layer], zero])
        vres = None
        if layer > 0:
            vres = (a_v1[layer - 1].astype(BF16), a_v2[layer - 1].astype(BF16), v_first)
        r, lw, k, v, kk, b, g = _rwkv_pre(
            x, vecs, a_w_in[layer].astype(BF16), a_w1[layer].astype(BF16), a_w2[layer].astype(BF16),
            a_a1[layer].astype(BF16), a_a2[layer].astype(BF16), seg, seg_t, vres)
        if layer == 0:
            v_first = v
        y = _wkv(r, lw, k, v, kk, b)
        flat = lambda a: a.reshape(t, d)
        x = _rwkv_post(flat(y), flat(r), flat(k), flat(v), flat(g), flat(x), vecs,
                       a_w_out[layer].astype(BF16), seg, seg_t).reshape(bsz, seq, d)

    x = x.reshape(t, d)
    pos = jnp.broadcast_to(positions.reshape(t, 1).astype(F32), (t, LANES))
    freq = _rope_tables()
    k_sh, vt_sh, *tables = _proj_rope(x, kv_norm.reshape(1, d), w_kv.astype(BF16), (pos, freq), 1.0,
                                      False, True, BF16)
    n_b = b_norm.shape[0]
    for j in range(n_b):
        layer = N_A + j
        lam_init = 0.8 - 0.6 * math.exp(-0.3 * layer)
        qt, gate = _proj_rope(x, b_norm[j].reshape(1, d), b_w_in[j].astype(BF16), tables,
                              B_QK ** -0.5 * math.log2(math.e), True, False, BF16)
        lam_vecs = jnp.zeros((8, LANES), F32).at[:4, :B_QK].set(
            jnp.stack([b_lq1[j], b_lk1[j], b_lq2[j], b_lk2[j]]))
        o = _diff_attn(qt, k_sh, vt_sh, lam_vecs, b_subln[j].reshape(1, B_V), lam_init, bsz, seq)
        x = _attn_post(o, gate, x, b_w_out[j].astype(BF16), final_norm.reshape(1, d), j == n_b - 1)
    return x.reshape(bsz, seq, d)
```
